```python
import math
import jax, jax.numpy as jnp
from jax import lax
import numpy as np

D_MODEL = 2048
BATCH = 4
SEQ = 4096
DEPTH = 4

f32 = jnp.float32
N_META = 16
HEAD_DIM = 128
N_MIXERS = 4
HEADS_PER_MIXER = (D_MODEL // HEAD_DIM) // N_MIXERS
GROUP_WIDTH = HEADS_PER_MIXER * HEAD_DIM
D_MIX = N_MIXERS * GROUP_WIDTH
D_FF = ((8 * D_MODEL // 3 + 255) // 256) * 256
EPS = 1e-6
ROPE_THETA = 10000.0
NEG_INF = -1e30
Q_BLOCK = 128
CHUNK = 64
DIFF_QK_DIM = HEAD_DIM // 2
GDN_DK = HEAD_DIM
GDN_DV = HEAD_DIM
CONV_K = 4
RET_DK = HEAD_DIM // 2
RET_DV = HEAD_DIM
IDX_HEADS = 16
IDX_DIM = 64
TOPK_MAX = 256

SPLIT_SIZES = (
    GROUP_WIDTH, GROUP_WIDTH, GROUP_WIDTH,
    GROUP_WIDTH, GROUP_WIDTH, GROUP_WIDTH, GROUP_WIDTH,
    HEADS_PER_MIXER, HEADS_PER_MIXER,
    HEADS_PER_MIXER * RET_DK, HEADS_PER_MIXER * RET_DK,
    GROUP_WIDTH, GROUP_WIDTH,
    GROUP_WIDTH, GROUP_WIDTH, GROUP_WIDTH,
    IDX_HEADS * IDX_DIM, IDX_DIM, IDX_HEADS,
)
SPLIT_POINTS = tuple(int(i) for i in np.cumsum(SPLIT_SIZES)[:-1])
N_IN = int(sum(SPLIT_SIZES))

kernel_name = 'hybrid_parallel_heads_trunk'


def rms_norm(x, g):
    xf = x.astype(f32)
    y = xf * lax.rsqrt(jnp.mean(xf * xf, axis=-1, keepdims=True) + EPS)
    return (y * g.astype(f32)).astype(x.dtype)


def l2_norm(x):
    return x * lax.rsqrt(jnp.sum(x * x, axis=-1, keepdims=True) + EPS)


def rope(x, pos):
    d = x.shape[-1]
    inv = ROPE_THETA ** (-jnp.arange(0, d, 2, dtype=f32) / d)
    ang = pos.astype(f32)[:, None] * inv[None, :]
    cos = jnp.cos(ang)[:, None, :]
    sin = jnp.sin(ang)[:, None, :]
    xf = x.astype(f32)
    x1, x2 = xf[..., : d // 2], xf[..., d // 2:]
    return jnp.concatenate([x1 * cos - x2 * sin, x2 * cos + x1 * sin], axis=-1).astype(x.dtype)


def left_pad(a, n):
    return jnp.pad(a, [(0, 0), (n, 0)] + [(0, 0)] * (a.ndim - 2))


def to_chunks(a, n):
    b = a.shape[0]
    return jnp.moveaxis(a.reshape((b, n, CHUNK) + a.shape[2:]), (1, 2), (0, 3))


def from_chunks(o):
    n, b, h, c, d = o.shape
    return o.transpose(1, 0, 3, 2, 4).reshape(b, n * c, h, d)


def causal_conv(x, w):
    c = x.shape[-1]
    return lax.conv_general_dilated(x, w[:, None, :], window_strides=(1,),
                                    padding=[(w.shape[0] - 1, 0)],
                                    dimension_numbers=('NWC', 'WIO', 'NWC'),
                                    feature_group_count=c)


def swiglu(x, w_in, w_out):
    gate, up = jnp.split(x @ w_in, 2, axis=-1)
    return (jax.nn.silu(gate) * up) @ w_out


def diff_attention(q, k, v, pos, q_norm, k_norm, lq1, lk1, lq2, lk2, out_norm, lambda_init):
    b, t, _ = q.shape
    h = HEADS_PER_MIXER
    q = rope(rms_norm(q.reshape(b, t, 2 * h, DIFF_QK_DIM), q_norm), pos)
    k = rope(rms_norm(k.reshape(b, t, 2 * h, DIFF_QK_DIM), k_norm), pos)
    v = v.reshape(b, t, h, HEAD_DIM)
    lam = (jnp.exp(jnp.sum(lq1.astype(f32) * lk1.astype(f32)))
           - jnp.exp(jnp.sum(lq2.astype(f32) * lk2.astype(f32))) + lambda_init)
    pad = Q_BLOCK - N_META
    qp, kp, vp = left_pad(q, pad), left_pad(k, pad), left_pad(v, pad)
    p_len = t + pad
    nb = p_len // Q_BLOCK
    qb = qp.reshape(b, nb, Q_BLOCK, 2 * h, DIFF_QK_DIM).swapaxes(0, 1)
    key_idx = jnp.arange(p_len)
    scale = DIFF_QK_DIM ** -0.5

    def block(args):
        qi, bi = args
        qpos = bi * Q_BLOCK + jnp.arange(Q_BLOCK)
        mask = (key_idx[None, :] <= qpos[:, None]) & (key_idx[None, :] >= pad)
        s = jnp.einsum('bqhd,bkhd->bhqk', qi, kp, preferred_element_type=f32) * scale
        p = jax.nn.softmax(jnp.where(mask, s, NEG_INF), axis=-1).reshape(b, h, 2, Q_BLOCK, p_len)
        pd = p[:, :, 0] - lam * p[:, :, 1]
        return jnp.einsum('bhqk,bkhd->bqhd', pd.astype(vp.dtype), vp)

    o = lax.map(block, (qb, jnp.arange(nb)))
    o = o.swapaxes(0, 1).reshape(b, p_len, h, HEAD_DIM)[:, pad:]
    o = rms_norm(o, out_norm) * (1.0 - lambda_init)
    return o.reshape(b, t, GROUP_WIDTH)


def gated_delta_net(q, k, v, z, beta_logit, a_logit, conv_w, a_log, dt_bias, out_norm):
    b, t, _ = q.shape
    h = HEADS_PER_MIXER
    qkv = jax.nn.silu(causal_conv(jnp.concatenate([q, k, v], axis=-1), conv_w))
    q, k, v = jnp.split(qkv.astype(f32), 3, axis=-1)
    q = l2_norm(q.reshape(b, t, h, GDN_DK)) * GDN_DK ** -0.5
    k = l2_norm(k.reshape(b, t, h, GDN_DK))
    v = v.reshape(b, t, h, GDN_DV)
    beta = jax.nn.sigmoid(beta_logit.astype(f32))
    g = -jnp.exp(a_log.astype(f32)) * jax.nn.softplus(a_logit.astype(f32) + dt_bias.astype(f32))
    pad = CHUNK - N_META
    n = (t + pad) // CHUNK
    xs = tuple(to_chunks(left_pad(a, pad), n) for a in (q, k, v, beta, g))
    tri = jnp.tril(jnp.ones((CHUNK, CHUNK), bool))
    strict = jnp.tril(jnp.ones((CHUNK, CHUNK), bool), -1)
    eye = jnp.eye(CHUNK, dtype=f32)

    def step(s, xc):
        qc, kc, vc, bc, gc = xc
        gcum = jnp.cumsum(gc, axis=-1)
        decay = jnp.where(tri, jnp.exp(jnp.minimum(gcum[..., :, None] - gcum[..., None, :], 0.0)), 0.0)
        kb = kc * bc[..., None]
        a_mat = jnp.where(strict, jnp.einsum('bhcd,bhsd->bhcs', kb, kc) * decay, 0.0)
        t_inv = lax.linalg.triangular_solve(eye + a_mat, jnp.broadcast_to(eye, a_mat.shape),
                                            left_side=True, lower=True, unit_diagonal=True)
        u = t_inv @ (vc * bc[..., None])
        w = t_inv @ (kb * jnp.exp(gcum)[..., None])
        v_new = u - w @ s
        attn = jnp.einsum('bhcd,bhsd->bhcs', qc, kc) * decay
        o = (qc * jnp.exp(gcum)[..., None]) @ s + attn @ v_new
        g_last = gcum[..., -1:]
        s = s * jnp.exp(g_last)[..., None] + jnp.einsum('bhcd,bhcv->bhdv', kc * jnp.exp(g_last - gcum)[..., None], v_new)
        return s, o

    _, o = lax.scan(step, jnp.zeros((b, h, GDN_DK, GDN_DV), f32), xs)
    o = from_chunks(o)[:, pad:]
    o = rms_norm(o, out_norm) * jax.nn.silu(z.reshape(b, t, h, GDN_DV).astype(f32))
    return o.reshape(b, t, GROUP_WIDTH).astype(z.dtype)


def retention(q, k, v, gate, pos, out_norm):
    b, t, _ = q.shape
    h = HEADS_PER_MIXER
    q = rope(q.reshape(b, t, h, RET_DK), pos).astype(f32)
    k = rope(k.reshape(b, t, h, RET_DK), pos).astype(f32) * RET_DK ** -0.5
    v = v.reshape(b, t, h, RET_DV).astype(f32)
    pad = CHUNK - N_META
    n = (t + pad) // CHUNK
    xs = tuple(to_chunks(left_pad(a, pad), n) for a in (q, k, v))
    log_g = jnp.log(1.0 - 2.0 ** (-5.0 - jnp.arange(h, dtype=f32)))
    i = jnp.arange(CHUNK, dtype=f32)
    dist = i[:, None] - i[None, :]
    dmat = jnp.where(dist >= 0, jnp.exp(log_g[:, None, None] * jnp.maximum(dist, 0.0)), 0.0)
    xi = jnp.exp(log_g[:, None] * (i + 1.0))[..., None]
    zeta = jnp.exp(log_g[:, None] * (CHUNK - 1.0 - i))[..., None]
    g_chunk = jnp.exp(log_g * CHUNK)[:, None, None]

    def step(r, xc):
        qc, kc, vc = xc
        inner = jnp.einsum('bhcd,bhsd->bhcs', qc, kc) * dmat
        o = inner @ vc + (qc * xi) @ r
        r = r * g_chunk + jnp.einsum('bhsd,bhsv->bhdv', kc * zeta, vc)
        return r, o

    _, o = lax.scan(step, jnp.zeros((b, h, RET_DK, RET_DV), f32), xs)
    o = rms_norm(from_chunks(o)[:, pad:], out_norm).reshape(b, t, GROUP_WIDTH)
    return (jax.nn.silu(gate.astype(f32)) * o).astype(gate.dtype)


def dsa_attention(q, k, v, iq, ik, iw, pos, q_norm, k_norm, n_keep):
    b, t, _ = q.shape
    h = HEADS_PER_MIXER
    q = rope(rms_norm(q.reshape(b, t, h, HEAD_DIM), q_norm), pos)
    k = rope(rms_norm(k.reshape(b, t, h, HEAD_DIM), k_norm), pos)
    v = v.reshape(b, t, h, HEAD_DIM)
    iq = rope(iq.reshape(b, t, IDX_HEADS, IDX_DIM), pos)
    ik = rope(ik[:, :, None, :], pos)[:, :, 0]
    pad = Q_BLOCK - N_META
    qp, kp, vp, iqp, ikp, iwp = (left_pad(a, pad) for a in (q, k, v, iq, ik, iw))
    p_len = t + pad
    nb = p_len // Q_BLOCK
    qb = qp.reshape(b, nb, Q_BLOCK, h, HEAD_DIM).swapaxes(0, 1)
    iqb = iqp.reshape(b, nb, Q_BLOCK, IDX_HEADS, IDX_DIM).swapaxes(0, 1)
    iwb = iwp.reshape(b, nb, Q_BLOCK, IDX_HEADS).swapaxes(0, 1)
    key_idx = jnp.arange(p_len)
    bidx = jnp.arange(b)[:, None, None]
    scale = HEAD_DIM ** -0.5

    def block(args):
        qi, iqi, iwi, bi = args
        qpos = bi * Q_BLOCK + jnp.arange(Q_BLOCK)
        mask = (key_idx[None, :] <= qpos[:, None]) & (key_idx[None, :] >= pad)
        rel = jax.nn.relu(jnp.einsum('bqhd,bkd->bqhk', iqi, ikp, preferred_element_type=f32))
        score = jnp.einsum('bqh,bqhk->bqk', iwi.astype(f32), rel)
        score = jnp.where(mask[None], score, -jnp.inf)
        _, sel = lax.top_k(score, n_keep)
        sel_ok = (sel <= qpos[None, :, None]) & (sel >= pad)
        k_sel = kp[bidx, sel]
        v_sel = vp[bidx, sel]
        s = jnp.einsum('bqhd,bqkhd->bhqk', qi, k_sel, preferred_element_type=f32) * scale
        p = jax.nn.softmax(jnp.where(sel_ok[:, None], s, NEG_INF), axis=-1)
        return jnp.einsum('bhqk,bqkhd->bqhd', p.astype(v_sel.dtype), v_sel)

    o = lax.map(block, (qb, iqb, iwb, jnp.arange(nb)))
    return o.swapaxes(0, 1).reshape(b, p_len, GROUP_WIDTH)[:, pad:]


def setup_inputs(seed: int = 0) -> dict:
    key = jax.random.key(seed)
    ks = jax.random.split(key, 26)

    def nrm(k, shape, scale):
        return jax.random.normal(k, shape, f32) * scale

    def gain(k, shape):
        return 1.0 + 0.02 * jax.random.normal(k, shape, f32)

    h = HEADS_PER_MIXER
    return {
        'x': nrm(ks[0], (BATCH, SEQ, D_MODEL), 1.0),
        'meta_tokens': nrm(ks[1], (N_META, D_MODEL), 1.0),
        'ffn1_norm': gain(ks[2], (DEPTH, D_MODEL)),
        'ffn1_w_in': nrm(ks[3], (DEPTH, D_MODEL, 2 * D_FF), D_MODEL ** -0.5),
        'ffn1_w_out': nrm(ks[4], (DEPTH, D_FF, D_MODEL), D_FF ** -0.5),
        'mix_norm': gain(ks[5], (DEPTH, D_MODEL)),
        'w_in': nrm(ks[6], (DEPTH, D_MODEL, N_IN), D_MODEL ** -0.5),
        'w_out': nrm(ks[7], (DEPTH, D_MIX, D_MODEL), D_MIX ** -0.5),
        'diff_q_norm': gain(ks[8], (DEPTH, DIFF_QK_DIM)),
        'diff_k_norm': gain(ks[9], (DEPTH, DIFF_QK_DIM)),
        'diff_lambda_q1': nrm(ks[10], (DEPTH, DIFF_QK_DIM), 0.1),
        'diff_lambda_k1': nrm(ks[11], (DEPTH, DIFF_QK_DIM), 0.1),
        'diff_lambda_q2': nrm(ks[12], (DEPTH, DIFF_QK_DIM), 0.1),
        'diff_lambda_k2': nrm(ks[13], (DEPTH, DIFF_QK_DIM), 0.1),
        'diff_out_norm': gain(ks[14], (DEPTH, HEAD_DIM)),
        'gdn_conv_w': nrm(ks[15], (DEPTH, CONV_K, 3 * GROUP_WIDTH), CONV_K ** -0.5),
        'gdn_a_log': jnp.log(jax.random.uniform(ks[16], (DEPTH, h), f32, 1.0, 16.0)),
        'gdn_dt_bias': nrm(ks[17], (DEPTH, h), 0.1),
        'gdn_out_norm': gain(ks[18], (DEPTH, GDN_DV)),
        'ret_out_norm': gain(ks[19], (DEPTH, RET_DV)),
        'dsa_q_norm': gain(ks[20], (DEPTH, HEAD_DIM)),
        'dsa_k_norm': gain(ks[21], (DEPTH, HEAD_DIM)),
        'ffn2_norm': gain(ks[22], (DEPTH, D_MODEL)),
        'ffn2_w_in': nrm(ks[23], (DEPTH, D_MODEL, 2 * D_FF), D_MODEL ** -0.5),
        'ffn2_w_out': nrm(ks[24], (DEPTH, D_FF, D_MODEL), D_FF ** -0.5),
    }


def reference(x, meta_tokens, ffn1_norm, ffn1_w_in, ffn1_w_out, mix_norm, w_in, w_out,
              diff_q_norm, diff_k_norm, diff_lambda_q1, diff_lambda_k1, diff_lambda_q2,
              diff_lambda_k2, diff_out_norm, gdn_conv_w, gdn_a_log, gdn_dt_bias, gdn_out_norm,
              ret_out_norm, dsa_q_norm, dsa_k_norm, ffn2_norm, ffn2_w_in, ffn2_w_out):
    b, seq, d = x.shape
    n_keep = min(TOPK_MAX, seq // 4)
    h = jnp.concatenate([jnp.broadcast_to(meta_tokens.astype(x.dtype)[None], (b, N_META, d)), x], axis=1)
    pos = jnp.arange(seq + N_META)
    for l in range(DEPTH):
        h = h + 0.5 * swiglu(rms_norm(h, ffn1_norm[l]), ffn1_w_in[l], ffn1_w_out[l])
        u = rms_norm(h, mix_norm[l])
        (a_q, a_k, a_v, b_q, b_k, b_v, b_z, b_beta, b_alpha, c_q, c_k, c_v, c_g,
         d_q, d_k, d_v, d_iq, d_ik, d_iw) = jnp.split(u @ w_in[l], SPLIT_POINTS, axis=-1)
        lambda_init = 0.8 - 0.6 * math.exp(-0.3 * l)
        y_a = diff_attention(a_q, a_k, a_v, pos, diff_q_norm[l], diff_k_norm[l], diff_lambda_q1[l],
                             diff_lambda_k1[l], diff_lambda_q2[l], diff_lambda_k2[l], diff_out_norm[l], lambda_init)
        y_b = gated_delta_net(b_q, b_k, b_v, b_z, b_beta, b_alpha, gdn_conv_w[l], gdn_a_log[l],
                              gdn_dt_bias[l], gdn_out_norm[l])
        y_c = retention(c_q, c_k, c_v, c_g, pos, ret_out_norm[l])
        y_d = dsa_attention(d_q, d_k, d_v, d_iq, d_ik, d_iw, pos, dsa_q_norm[l], dsa_k_norm[l], n_keep)
        h = h + jnp.concatenate([y_a, y_b, y_c, y_d], axis=-1) @ w_out[l]
        h = h + 0.5 * swiglu(rms_norm(h, ffn2_norm[l]), ffn2_w_in[l], ffn2_w_out[l])
    return h[:, N_META:]
```

```python
import functools
import math

import jax
import jax.numpy as jnp
import numpy as np
from jax import lax
from jax.experimental import pallas as pl
from jax.experimental.pallas import tpu as pltpu

f32 = jnp.float32
bf16 = jnp.bfloat16
HIGHEST = lax.Precision.HIGHEST

N_META = 16
HEAD_DIM = 128
HEADS = 4
GROUP = HEADS * HEAD_DIM
EPS = 1e-6
ROPE_THETA = 10000.0
NEG = -1e30
TOPK_MAX = 256
CONV_K = 4
IDX_HEADS = 16
IDX_DIM = 64
CHUNK = 64
LANES = 128
ATT_TILE = 384
VMEM_LIMIT = 56 * 1024 * 1024

NP = 8192
COL_A = 0
COL_B = 1536
COL_C = 3584
COL_D = 5120
COL_IK = 7680
COL_SM = 7808
SM_IW, SM_BETA, SM_ALPHA = 0, 16, 20

_NT = (((1,), (1,)), ((), ()))


def _cparams(sem):
    return pltpu.CompilerParams(dimension_semantics=sem, vmem_limit_bytes=VMEM_LIMIT)


def _row_tile(rows, target):
    best = 8
    for t in range(8, target + 1, 8):
        if rows % t == 0:
            best = t
    return best


def _silu(x):
    return x * jax.nn.sigmoid(x)


def _rms(x, g):
    return x * lax.rsqrt(jnp.mean(x * x, axis=-1, keepdims=True) + EPS) * g


def _ffn_kernel(h_ref, g_ref, wg_ref, wu_ref, wo_ref, o_ref, xn_ref):
    @pl.when(pl.program_id(1) == 0)
    def _():
        x = h_ref[...]
        xn_ref[...] = _rms(x, g_ref[...]).astype(bf16)
        o_ref[...] = x

    xn = xn_ref[...]
    gate = jnp.dot(xn, wg_ref[...], preferred_element_type=f32)
    up = jnp.dot(xn, wu_ref[...], preferred_element_type=f32)
    act = (_silu(gate) * up).astype(bf16)
    o_ref[...] += 0.5 * jnp.dot(act, wo_ref[...], preferred_element_type=f32)


def _ffn(h, g, w_in, w_out):
    rows, d = h.shape
    ff = w_out.shape[0]
    tm = _row_tile(rows, 768)
    tf = 512
    nf = ff // tf
    return pl.pallas_call(
        _ffn_kernel,
        grid=(rows // tm, nf),
        in_specs=[
            pl.BlockSpec((tm, d), lambda i, j: (i, 0)),
            pl.BlockSpec((1, d), lambda i, j: (0, 0)),
            pl.BlockSpec((d, tf), lambda i, j: (0, j)),
            pl.BlockSpec((d, tf), lambda i, j: (0, j + nf)),
            pl.BlockSpec((tf, d), lambda i, j: (j, 0)),
        ],
        out_specs=pl.BlockSpec((tm, d), lambda i, j: (i, 0)),
        out_shape=jax.ShapeDtypeStruct((rows, d), f32),
        scratch_shapes=[pltpu.VMEM((tm, d), bf16)],
        compiler_params=_cparams(("parallel", "arbitrary")),
        name="ffn",
    )(h, g, w_in, w_in, w_out)


def _proj_kernel(h_ref, g_ref, w_ref, o_ref, xn_ref):
    @pl.when(pl.program_id(1) == 0)
    def _():
        xn_ref[...] = _rms(h_ref[...], g_ref[...]).astype(bf16)

    o_ref[...] = jnp.dot(xn_ref[...], w_ref[...], preferred_element_type=f32)


def _proj(h, g, w):
    rows, d = h.shape
    n = w.shape[1]
    tm = _row_tile(rows, 768)
    tn = 1024
    return pl.pallas_call(
        _proj_kernel,
        grid=(rows // tm, n // tn),
        in_specs=[
            pl.BlockSpec((tm, d), lambda i, j: (i, 0)),
            pl.BlockSpec((1, d), lambda i, j: (0, 0)),
            pl.BlockSpec((d, tn), lambda i, j: (0, j)),
        ],
        out_specs=pl.BlockSpec((tm, tn), lambda i, j: (i, j)),
        out_shape=jax.ShapeDtypeStruct((rows, n), f32),
        scratch_shapes=[pltpu.VMEM((tm, d), bf16)],
        compiler_params=_cparams(("parallel", "arbitrary")),
        name="proj",
    )(h, g, w)


def _outproj_kernel(h_ref, ya_ref, yb_ref, yc_ref, yd_ref, w_ref, o_ref):
    acc = h_ref[...]
    for n, y_ref in enumerate((ya_ref, yb_ref, yc_ref, yd_ref)):
        acc = acc + jnp.dot(y_ref[...], w_ref[n * GROUP:(n + 1) * GROUP, :],
                            preferred_element_type=f32)
    o_ref[...] = acc


def _outproj(h, ys, w):
    rows, d = h.shape
    tm = _row_tile(rows, 768)
    y_spec = pl.BlockSpec((tm, GROUP), lambda i: (i, 0))
    return pl.pallas_call(
        _outproj_kernel,
        grid=(rows // tm,),
        in_specs=[pl.BlockSpec((tm, d), lambda i: (i, 0)), y_spec, y_spec, y_spec, y_spec,
                  pl.BlockSpec((w.shape[0], d), lambda i: (0, 0))],
        out_specs=pl.BlockSpec((tm, d), lambda i: (i, 0)),
        out_shape=jax.ShapeDtypeStruct((rows, d), f32),
        compiler_params=_cparams(("parallel",)),
        name="outproj",
    )(h, *ys, w)


def _rope64(x, c, sa, sb):
    return x * c + pltpu.roll(x, LANES - 32, 1) * sa + pltpu.roll(x, 32, 1) * sb


def _rope128(x, c, s):
    return x * c + pltpu.roll(x, 64, 1) * s


def _group_mean_matrix():
    r = lax.broadcasted_iota(jnp.int32, (LANES, LANES), 0) >> 6
    c = lax.broadcasted_iota(jnp.int32, (LANES, LANES), 1) >> 6
    return jnp.where(r == c, 1.0 / 64.0, 0.0).astype(f32)


def _prep_a_kernel(q_ref, k_ref, v_ref, c_ref, sa_ref, sb_ref, gq_ref, gk_ref,
                   qz_ref, ko_ref, vo_ref):
    tm = q_ref.shape[0]
    lane = lax.broadcasted_iota(jnp.int32, (tm, LANES), 1)
    gmat = _group_mean_matrix()
    c, sa, sb = c_ref[...], sa_ref[...], sb_ref[...]

    def norm_rope(x, g):
        ms = jnp.dot(x * x, gmat, precision=HIGHEST, preferred_element_type=f32)
        return _rope64(x * lax.rsqrt(ms + EPS) * g, c, sa, sb)

    for h in range(HEADS):
        sl = slice(h * HEAD_DIM, (h + 1) * HEAD_DIM)
        q = norm_rope(q_ref[:, sl], gq_ref[...]) * (64 ** -0.5)
        qz_ref[h, 0] = jnp.where(lane < 64, q, 0.0).astype(bf16)
        qz_ref[h, 1] = jnp.where(lane >= 64, q, 0.0).astype(bf16)
        ko_ref[h] = norm_rope(k_ref[:, sl], gk_ref[...]).astype(bf16)
        vo_ref[h] = v_ref[:, sl].astype(bf16)


def _prep_a(proj, tabs64, gq, gk):
    b, tp, _ = proj.shape
    tm = ATT_TILE
    cb = COL_A // GROUP
    tab_spec = pl.BlockSpec((tm, LANES), lambda bi, i: (i, 0))
    g_spec = pl.BlockSpec((1, LANES), lambda bi, i: (0, 0))
    hspec = pl.BlockSpec((None, HEADS, tm, HEAD_DIM), lambda bi, i: (bi, 0, i, 0))
    return pl.pallas_call(
        _prep_a_kernel,
        grid=(b, tp // tm),
        in_specs=[
            pl.BlockSpec((None, tm, GROUP), lambda bi, i: (bi, i, cb)),
            pl.BlockSpec((None, tm, GROUP), lambda bi, i: (bi, i, cb + 1)),
            pl.BlockSpec((None, tm, GROUP), lambda bi, i: (bi, i, cb + 2)),
            tab_spec, tab_spec, tab_spec, g_spec, g_spec,
        ],
        out_specs=[
            pl.BlockSpec((None, HEADS, 2, tm, HEAD_DIM), lambda bi, i: (bi, 0, 0, i, 0)),
            hspec, hspec,
        ],
        out_shape=[
            jax.ShapeDtypeStruct((b, HEADS, 2, tp, HEAD_DIM), bf16),
            jax.ShapeDtypeStruct((b, HEADS, tp, HEAD_DIM), bf16),
            jax.ShapeDtypeStruct((b, HEADS, tp, HEAD_DIM), bf16),
        ],
        compiler_params=_cparams(("parallel", "parallel")),
        name="prep_a",
    )(proj, proj, proj, *tabs64, gq, gk)


def _attn_a_kernel(qz_ref, k_ref, v_ref, lq1_ref, lk1_ref, lq2_ref, lk2_ref, gn_ref, o_ref,
                   *, tq, lambda_init):
    i = pl.program_id(2)
    q = qz_ref[...].reshape(2 * tq, HEAD_DIM)
    row = lax.broadcasted_iota(jnp.int32, (2 * tq, tq), 0)
    col = lax.broadcasted_iota(jnp.int32, (2 * tq, tq), 1)
    causal = col <= jnp.where(row >= tq, row - tq, row)

    def step(j, carry, masked):
        m, l, acc = carry
        start = pl.multiple_of(j * tq, tq)
        kb = k_ref[pl.ds(start, tq), :]
        vb = v_ref[pl.ds(start, tq), :]
        s = lax.dot_general(q, kb, _NT, preferred_element_type=f32)
        if masked:
            s = jnp.where(causal, s, NEG)
        m_new = jnp.maximum(m, jnp.max(s, axis=-1, keepdims=True))
        alpha = jnp.exp(m - m_new)
        p = jnp.exp(s - m_new)
        l = alpha * l + jnp.sum(p, axis=-1, keepdims=True)
        acc = alpha * acc + jnp.dot(p.astype(bf16), vb, preferred_element_type=f32)
        return m_new, l, acc

    init = (jnp.full((2 * tq, 1), NEG, f32), jnp.zeros((2 * tq, 1), f32),
            jnp.zeros((2 * tq, HEAD_DIM), f32))
    carry = lax.fori_loop(0, i, lambda j, c: step(j, c, False), init)
    m, l, acc = step(i, carry, True)
    o = acc / l
    lam = (jnp.exp(jnp.sum(lq1_ref[...] * lk1_ref[...], axis=-1, keepdims=True))
           - jnp.exp(jnp.sum(lq2_ref[...] * lk2_ref[...], axis=-1, keepdims=True))
           + lambda_init)
    od = o[:tq] - lam * o[tq:]
    o_ref[...] = (_rms(od, gn_ref[...]) * (1.0 - lambda_init)).astype(bf16)


def _attn_a(qz, k, v, lq1, lk1, lq2, lk2, gn, lambda_init):
    b, _, _, tp, _ = qz.shape
    tq = ATT_TILE
    kv_spec = pl.BlockSpec((None, None, tp, HEAD_DIM), lambda bi, h, i: (bi, h, 0, 0))
    l_spec = pl.BlockSpec((1, 64), lambda bi, h, i: (0, 0))
    return pl.pallas_call(
        functools.partial(_attn_a_kernel, tq=tq, lambda_init=lambda_init),
        grid=(b, HEADS, tp // tq),
        in_specs=[
            pl.BlockSpec((None, None, 2, tq, HEAD_DIM), lambda bi, h, i: (bi, h, 0, i, 0)),
            kv_spec, kv_spec, l_spec, l_spec, l_spec, l_spec,
            pl.BlockSpec((1, HEAD_DIM), lambda bi, h, i: (0, 0)),
        ],
        out_specs=pl.BlockSpec((None, tq, HEAD_DIM), lambda bi, h, i: (bi, i, h)),
        out_shape=jax.ShapeDtypeStruct((b, tp, GROUP), bf16),
        compiler_params=_cparams(("parallel", "parallel", "arbitrary")),
        name="attn_a",
    )(qz, k, v, lq1, lk1, lq2, lk2, gn)


def _gdn_kernel(qkv_ref, z_ref, sm_ref, cw_ref, alog_ref, dtb_ref, gn_ref, o_ref,
                xx_ref, s_ref, *, tb):
    w3 = 3 * GROUP

    @pl.when(pl.program_id(1) == 0)
    def _():
        xx_ref[0:8, :] = jnp.zeros((8, w3), f32)
        s_ref[...] = jnp.zeros_like(s_ref)

    x = qkv_ref[...]
    xx_ref[8:tb + 8, :] = x
    cw = cw_ref[...]
    y = x * cw[CONV_K - 1:CONV_K, :]
    for s in range(1, CONV_K):
        y = y + xx_ref[pl.ds(8 - s, tb), :] * cw[CONV_K - 1 - s:CONV_K - s, :]
    xx_ref[0:8, :] = x[tb - 8:, :]
    y = _silu(y)

    sm = sm_ref[...]
    beta_all = jax.nn.sigmoid(sm)
    g_all = -jnp.exp(alog_ref[...]) * jax.nn.softplus(sm + dtb_ref[...])

    ri = lax.broadcasted_iota(jnp.int32, (CHUNK, CHUNK), 0)
    ci = lax.broadcasted_iota(jnp.int32, (CHUNK, CHUNK), 1)
    tri = ri >= ci
    strict = ri > ci
    lmat = jnp.where(tri, 1.0, 0.0).astype(f32)
    eye = jnp.where(ri == ci, 1.0, 0.0).astype(f32)
    gn = gn_ref[...]

    def hdot(a, b):
        return jnp.dot(a, b, precision=HIGHEST, preferred_element_type=f32)

    for c in range(tb // CHUNK):
        rows = slice(c * CHUNK, (c + 1) * CHUNK)
        for h in range(HEADS):
            sl = slice(h * HEAD_DIM, (h + 1) * HEAD_DIM)
            qc = y[rows, h * HEAD_DIM:(h + 1) * HEAD_DIM]
            kc = y[rows, GROUP + h * HEAD_DIM:GROUP + (h + 1) * HEAD_DIM]
            vc = y[rows, 2 * GROUP + h * HEAD_DIM:2 * GROUP + (h + 1) * HEAD_DIM]
            qc = qc * lax.rsqrt(jnp.sum(qc * qc, axis=-1, keepdims=True) + EPS) * (HEAD_DIM ** -0.5)
            kc = kc * lax.rsqrt(jnp.sum(kc * kc, axis=-1, keepdims=True) + EPS)
            bc = beta_all[rows, SM_BETA + h:SM_BETA + h + 1]
            gc = jnp.broadcast_to(g_all[rows, SM_ALPHA + h:SM_ALPHA + h + 1], (CHUNK, LANES))
            gcum = hdot(lmat, gc)
            diff = hdot(lmat, jnp.where(strict, gc[:, :CHUNK], 0.0))
            decay = jnp.where(tri, jnp.exp(diff), 0.0)
            egc = jnp.exp(gcum)
            glast = gcum[CHUNK - 1:CHUNK, :]
            kb = kc * bc
            kk = lax.dot_general(kb.astype(bf16), kc.astype(bf16), _NT, preferred_element_type=f32)
            a_mat = jnp.where(strict, kk * decay, 0.0)
            t_inv = eye - a_mat
            pw = hdot(a_mat, a_mat)
            n_sq = int(math.log2(CHUNK)) - 1
            for it in range(n_sq):
                t_inv = t_inv + hdot(t_inv, pw)
                if it + 1 < n_sq:
                    pw = hdot(pw, pw)
            rhs = jnp.concatenate([vc * bc, kb * egc], axis=1).astype(bf16)
            uw = jnp.dot(t_inv.astype(bf16), rhs, preferred_element_type=f32)
            u, w = uw[:, :HEAD_DIM], uw[:, HEAD_DIM:]
            attn = lax.dot_general(qc.astype(bf16), kc.astype(bf16), _NT,
                                   preferred_element_type=f32) * decay
            state = s_ref[h]
            sb = state.astype(bf16)
            v_new = u - jnp.dot(w.astype(bf16), sb, preferred_element_type=f32)
            vnb = v_new.astype(bf16)
            o = (jnp.dot((qc * egc).astype(bf16), sb, preferred_element_type=f32)
                 + jnp.dot(attn.astype(bf16), vnb, preferred_element_type=f32))
            kd = (kc * jnp.exp(glast - gcum)).T.astype(bf16)
            s_ref[h] = state * jnp.exp(glast) + jnp.dot(kd, vnb, preferred_element_type=f32)
            zh = z_ref[rows, sl]
            o_ref[rows, sl] = (_rms(o, gn) * _silu(zh)).astype(bf16)


def _gdn(proj, conv_w, alog_row, dtb_row, gn):
    b, tp, _ = proj.shape
    tb = 128
    w3 = 3 * GROUP
    return pl.pallas_call(
        functools.partial(_gdn_kernel, tb=tb),
        grid=(b, tp // tb),
        in_specs=[
            pl.BlockSpec((None, tb, w3), lambda bi, t: (bi, t, COL_B // w3)),
            pl.BlockSpec((None, tb, GROUP), lambda bi, t: (bi, t, (COL_B + w3) // GROUP)),
            pl.BlockSpec((None, tb, LANES), lambda bi, t: (bi, t, COL_SM // LANES)),
            pl.BlockSpec((CONV_K, w3), lambda bi, t: (0, 0)),
            pl.BlockSpec((1, LANES), lambda bi, t: (0, 0)),
            pl.BlockSpec((1, LANES), lambda bi, t: (0, 0)),
            pl.BlockSpec((1, HEAD_DIM), lambda bi, t: (0, 0)),
        ],
        out_specs=pl.BlockSpec((None, tb, GROUP), lambda bi, t: (bi, t, 0)),
        out_shape=jax.ShapeDtypeStruct((b, tp, GROUP), bf16),
        scratch_shapes=[pltpu.VMEM((tb + 8, w3), f32), pltpu.VMEM((HEADS, HEAD_DIM, HEAD_DIM), f32)],
        compiler_params=_cparams(("parallel", "arbitrary")),
        name="gdn",
    )(proj, proj, proj, conv_w, alog_row, dtb_row, gn)


def _ret_kernel(qk_ref, v_ref, g_ref, c_ref, sa_ref, sb_ref, dm_ref, xi_ref, zt_ref, gch_ref,
                gn_ref, o_ref, r_ref, *, tc):
    @pl.when(pl.program_id(1) == 0)
    def _():
        r_ref[...] = jnp.zeros_like(r_ref)

    lane = lax.broadcasted_iota(jnp.int32, (tc, LANES), 1)
    c, sa, sb = c_ref[...], sa_ref[...], sb_ref[...]
    gn = gn_ref[...]
    for pair in range(HEADS // 2):
        qp = _rope64(qk_ref[:, pair * LANES:(pair + 1) * LANES], c, sa, sb)
        kp = _rope64(qk_ref[:, 2 * LANES + pair * LANES:2 * LANES + (pair + 1) * LANES],
                     c, sa, sb) * (64 ** -0.5)
        for half in range(2):
            h = 2 * pair + half
            sl = slice(h * HEAD_DIM, (h + 1) * HEAD_DIM)
            own = (lane >= 64) if half else (lane < 64)
            qm = jnp.where(own, qp, 0.0)
            km = jnp.where(own, kp, 0.0)
            vb = v_ref[:, sl].astype(bf16)
            inner = lax.dot_general(qm.astype(bf16), km.astype(bf16), _NT,
                                    preferred_element_type=f32) * dm_ref[h]
            r = r_ref[h]
            o = (jnp.dot(inner.astype(bf16), vb, preferred_element_type=f32)
                 + jnp.dot((qm * xi_ref[h]).astype(bf16), r.astype(bf16),
                           preferred_element_type=f32))
            kz = (km * zt_ref[h]).T.astype(bf16)
            r_ref[h] = r * gch_ref[h] + jnp.dot(kz, vb, preferred_element_type=f32)
            o_ref[:, sl] = (_silu(g_ref[:, sl]) * _rms(o, gn)).astype(bf16)


def _retention(proj, tabs64, consts, gn):
    b, tp, _ = proj.shape
    tc = ATT_TILE
    dmat, xi, zeta, gch = consts
    tab_spec = pl.BlockSpec((tc, LANES), lambda bi, t: (t, 0))
    cb = COL_C // GROUP
    full3 = lambda a: pl.BlockSpec(a.shape, lambda bi, t: (0, 0, 0))
    return pl.pallas_call(
        functools.partial(_ret_kernel, tc=tc),
        grid=(b, tp // tc),
        in_specs=[
            pl.BlockSpec((None, tc, GROUP), lambda bi, t: (bi, t, cb)),
            pl.BlockSpec((None, tc, GROUP), lambda bi, t: (bi, t, cb + 1)),
            pl.BlockSpec((None, tc, GROUP), lambda bi, t: (bi, t, cb + 2)),
            tab_spec, tab_spec, tab_spec,
            full3(dmat), full3(xi), full3(zeta), full3(gch),
            pl.BlockSpec((1, HEAD_DIM), lambda bi, t: (0, 0)),
        ],
        out_specs=pl.BlockSpec((None, tc, GROUP), lambda bi, t: (bi, t, 0)),
        out_shape=jax.ShapeDtypeStruct((b, tp, GROUP), bf16),
        scratch_shapes=[pltpu.VMEM((HEADS, HEAD_DIM, HEAD_DIM), f32)],
        compiler_params=_cparams(("parallel", "arbitrary")),
        name="retention",
    )(proj, proj, proj, *tabs64, dmat, xi, zeta, gch, gn)


def _retention_consts(tc):
    log_g = jnp.log(1.0 - 2.0 ** (-5.0 - jnp.arange(HEADS, dtype=f32)))
    i = jnp.arange(tc, dtype=f32)
    dist = i[:, None] - i[None, :]
    dmat = jnp.where(dist >= 0, jnp.exp(log_g[:, None, None] * jnp.maximum(dist, 0.0)), 0.0)
    ones = jnp.ones((1, 1, LANES), f32)
    xi = jnp.exp(log_g[:, None] * (i + 1.0))[..., None] * ones
    zeta = jnp.exp(log_g[:, None] * (tc - 1.0 - i))[..., None] * ones
    gch = jnp.exp(log_g * tc)[:, None, None] * ones
    return dmat, xi, zeta, gch


def _prep_d_kernel(q_ref, k_ref, v_ref, iq0_ref, iq1_ref, ik_ref, c64_ref, sa_ref, sb_ref,
                   c128_ref, s128_ref, gq_ref, gk_ref,
                   qo_ref, ko_ref, vo_ref, iqo_ref, iko_ref):
    c64, sa, sb = c64_ref[...], sa_ref[...], sb_ref[...]
    c128, s128 = c128_ref[...], s128_ref[...]
    for h in range(HEADS):
        sl = slice(h * HEAD_DIM, (h + 1) * HEAD_DIM)
        q = _rope128(_rms(q_ref[:, sl], gq_ref[...]), c128, s128) * (HEAD_DIM ** -0.5)
        qo_ref[h] = q.astype(bf16)
        ko_ref[h] = _rope128(_rms(k_ref[:, sl], gk_ref[...]), c128, s128).astype(bf16)
        vo_ref[h] = v_ref[:, sl].astype(bf16)
    for p in range(IDX_HEADS // 2):
        src = iq0_ref if p < 4 else iq1_ref
        x = _rope64(src[:, (p % 4) * LANES:(p % 4 + 1) * LANES], c64, sa, sb)
        iqo_ref[2 * p] = x[:, :IDX_DIM].astype(bf16)
        iqo_ref[2 * p + 1] = x[:, IDX_DIM:].astype(bf16)
    iko_ref[...] = _rope64(ik_ref[...], c64, sa, sb)[:, :IDX_DIM].astype(bf16)


def _prep_d(proj, tabs64, tabs128, gq, gk):
    b, tp, _ = proj.shape
    tm = ATT_TILE
    cb = COL_D // GROUP
    tab_spec = pl.BlockSpec((tm, LANES), lambda bi, i: (i, 0))
    g_spec = pl.BlockSpec((1, LANES), lambda bi, i: (0, 0))
    col = lambda k: pl.BlockSpec((None, tm, GROUP), lambda bi, i: (bi, i, cb + k))
    hspec = pl.BlockSpec((None, HEADS, tm, HEAD_DIM), lambda bi, i: (bi, 0, i, 0))
    hshape = jax.ShapeDtypeStruct((b, HEADS, tp, HEAD_DIM), bf16)
    return pl.pallas_call(
        _prep_d_kernel,
        grid=(b, tp // tm),
        in_specs=[col(0), col(1), col(2), col(3), col(4),
                  pl.BlockSpec((None, tm, LANES), lambda bi, i: (bi, i, COL_IK // LANES)),
                  tab_spec, tab_spec, tab_spec, tab_spec, tab_spec, g_spec, g_spec],
        out_specs=[hspec, hspec, hspec,
                   pl.BlockSpec((None, IDX_HEADS, tm, IDX_DIM), lambda bi, i: (bi, 0, i, 0)),
                   pl.BlockSpec((None, tm, IDX_DIM), lambda bi, i: (bi, i, 0))],
        out_shape=[hshape, hshape, hshape,
                   jax.ShapeDtypeStruct((b, IDX_HEADS, tp, IDX_DIM), bf16),
                   jax.ShapeDtypeStruct((b, tp, IDX_DIM), bf16)],
        compiler_params=_cparams(("parallel", "parallel")),
        name="prep_d",
    )(proj, proj, proj, proj, proj, proj, *tabs64, *tabs128, gq, gk)


def _key_to_f32(key):
    return lax.bitcast_convert_type(jnp.where(key < 0, key ^ 0x7FFFFFFF, key), f32)


def _dsa_kernel(iq_ref, iw_ref, ik_ref, q_ref, k_ref, v_ref, o_ref, sc_ref, cs_ref,
                *, tq, n_keep):
    i = pl.program_id(1)
    nkb = i + 1
    row = lax.broadcasted_iota(jnp.int32, (tq, tq), 0)
    col = lax.broadcasted_iota(jnp.int32, (tq, tq), 1)
    causal = col <= row
    iw = iw_ref[...]
    kf = float(n_keep)
    int_min = -2 ** 31
    neg_inf_key = int_min + 0x7FFFFF

    def scores(j, masked):
        start = pl.multiple_of(j * tq, tq)
        ikb = ik_ref[pl.ds(start, tq), :]
        sc = jnp.zeros((tq, tq), f32)
        for hh in range(IDX_HEADS):
            s = lax.dot_general(iq_ref[hh], ikb, _NT, preferred_element_type=f32)
            sc = sc + iw[:, SM_IW + hh:SM_IW + hh + 1] * jnp.maximum(s, 0.0)
        if masked:
            sc = jnp.where(causal, sc, -jnp.inf)
        sc_ref[j] = sc

    def _scores_body(j, carry):
        scores(j, False)
        return carry

    lax.fori_loop(0, i, _scores_body, 0)
    scores(i, True)

    def lane_fold(x):
        out = x[:, :LANES]
        for t in range(1, tq // LANES):
            out = out + x[:, t * LANES:(t + 1) * LANES]
        return out

    def count(pred):
        def body(j, acc):
            return acc + lane_fold(jnp.where(pred(sc_ref[j], j), 1.0, 0.0))
        acc = lax.fori_loop(0, nkb, body, jnp.zeros((tq, LANES), f32))
        return jnp.sum(acc, axis=-1, keepdims=True)

    def bisect(it, ans):
        cand = ans + lax.shift_left(jnp.int32(1), 31 - it)
        cand_f = _key_to_f32(cand)
        cnt = count(lambda blk, j: blk >= cand_f)
        return jnp.where(cnt >= kf, cand, ans)

    ans = lax.fori_loop(0, 32, bisect, jnp.full((tq, 1), int_min, jnp.int32))
    thr_key = jnp.maximum(ans, neg_inf_key)
    thr = _key_to_f32(thr_key)

    n_ge = count(lambda blk, j: blk >= thr)
    n_gt = count(lambda blk, j: blk > thr)
    need = (n_ge > kf) & (thr_key > neg_inf_key)
    big = float(2 ** 30)
    cs_ref[...] = jnp.full((tq, LANES), big, f32)

    @pl.when(jnp.max(jnp.where(need, 1.0, 0.0)) > 0.0)
    def _():
        budget = kf - n_gt

        def tie_bisect(it, lo):
            cand = lo + lax.shift_left(jnp.int32(1), 14 - it).astype(f32)

            def pred(blk, j):
                colg = (col + j * tq).astype(f32)
                return jnp.where(blk == thr, colg, big) < cand
            cnt = count(pred)
            return jnp.where(cnt <= budget - 1.0, cand, lo)

        cstar = lax.fori_loop(0, 15, tie_bisect, jnp.zeros((tq, 1), f32))
        cs_ref[...] = jnp.broadcast_to(jnp.where(need, cstar, big), (tq, LANES))

    cstar = cs_ref[:, 0:1]

    def attend(j, carry, masked):
        start = pl.multiple_of(j * tq, tq)
        blk = sc_ref[j]
        colg = (col + j * tq).astype(f32)
        sel = (blk > thr) | ((blk == thr) & (colg <= cstar))
        if masked:
            sel = sel & causal
        out = []
        for h in range(HEADS):
            m, l, acc = carry[h]
            kb = k_ref[h, pl.ds(start, tq), :]
            vb = v_ref[h, pl.ds(start, tq), :]
            s = lax.dot_general(q_ref[h], kb, _NT, preferred_element_type=f32)
            s = jnp.where(sel, s, NEG)
            m_new = jnp.maximum(m, jnp.max(s, axis=-1, keepdims=True))
            alpha = jnp.exp(m - m_new)
            p = jnp.where(sel, jnp.exp(s - m_new), 0.0)
            l = alpha * l + jnp.sum(p, axis=-1, keepdims=True)
            acc = alpha * acc + jnp.dot(p.astype(bf16), vb, preferred_element_type=f32)
            out.append((m_new, l, acc))
        return tuple(out)

    init = tuple((jnp.full((tq, 1), NEG, f32), jnp.zeros((tq, 1), f32),
                  jnp.zeros((tq, HEAD_DIM), f32)) for _ in range(HEADS))
    carry = lax.fori_loop(0, i, lambda j, c: attend(j, c, False), init)
    carry = attend(i, carry, True)
    for h in range(HEADS):
        _, l, acc = carry[h]
        o_ref[:, h * HEAD_DIM:(h + 1) * HEAD_DIM] = (acc / l).astype(bf16)


def _dsa(iq, proj, ik, q, k, v, n_keep):
    b, _, tp, _ = q.shape
    tq = ATT_TILE
    assert n_keep <= tq
    nq = tp // tq
    kv_spec = pl.BlockSpec((None, HEADS, tp, HEAD_DIM), lambda bi, i: (bi, 0, 0, 0))
    return pl.pallas_call(
        functools.partial(_dsa_kernel, tq=tq, n_keep=n_keep),
        grid=(b, nq),
        in_specs=[
            pl.BlockSpec((None, IDX_HEADS, tq, IDX_DIM), lambda bi, i: (bi, 0, i, 0)),
            pl.BlockSpec((None, tq, LANES), lambda bi, i: (bi, i, COL_SM // LANES)),
            pl.BlockSpec((None, tp, IDX_DIM), lambda bi, i: (bi, 0, 0)),
            pl.BlockSpec((None, HEADS, tq, HEAD_DIM), lambda bi, i: (bi, 0, i, 0)),
            kv_spec, kv_spec,
        ],
        out_specs=pl.BlockSpec((None, tq, GROUP), lambda bi, i: (bi, i, 0)),
        out_shape=jax.ShapeDtypeStruct((b, tp, GROUP), bf16),
        scratch_shapes=[pltpu.VMEM((nq, tq, tq), f32), pltpu.VMEM((tq, LANES), f32)],
        compiler_params=_cparams(("parallel", "arbitrary")),
        name="dsa",
    )(iq, proj, ik, q, k, v)


def _rope_tables(tp):
    pos = jnp.arange(tp, dtype=f32)

    def cs(d):
        inv = ROPE_THETA ** (-jnp.arange(0, d, 2, dtype=f32) / d)
        ang = pos[:, None] * inv[None, :]
        return jnp.cos(ang), jnp.sin(ang)

    c, s = cs(64)
    z = jnp.zeros_like(s)
    tabs64 = (jnp.tile(jnp.concatenate([c, c], axis=1), (1, 2)),
              jnp.tile(jnp.concatenate([-s, z], axis=1), (1, 2)),
              jnp.tile(jnp.concatenate([z, s], axis=1), (1, 2)))
    c, s = cs(128)
    tabs128 = (jnp.concatenate([c, c], axis=1), jnp.concatenate([-s, s], axis=1))
    return tabs64, tabs128


def _permute_w_in(w):
    d = w.shape[0]
    z = lambda n: jnp.zeros((d, n), w.dtype)
    o_beta = 7 * GROUP
    o_c = o_beta + 2 * HEADS
    o_ik = o_c + 8 * GROUP
    o_iw = o_ik + IDX_DIM
    end = o_iw + IDX_HEADS
    return jnp.concatenate([
        w[:, :o_beta], w[:, o_c:o_ik], w[:, o_ik:o_iw], z(LANES - IDX_DIM),
        w[:, o_iw:end], w[:, o_beta:o_c], z(NP - COL_SM - IDX_HEADS - 2 * HEADS)], axis=1)


def _lane_row(v, offset):
    return jnp.zeros((1, LANES), f32).at[0, offset:offset + v.shape[0]].set(v.astype(f32))


def kernel(x, meta_tokens, ffn1_norm, ffn1_w_in, ffn1_w_out, mix_norm, w_in, w_out, diff_q_norm, diff_k_norm, diff_lambda_q1, diff_lambda_k1, diff_lambda_q2, diff_lambda_k2, diff_out_norm, gdn_conv_w, gdn_a_log, gdn_dt_bias, gdn_out_norm, ret_out_norm, dsa_q_norm, dsa_k_norm, ffn2_norm, ffn2_w_in, ffn2_w_out):
    b, seq, d = x.shape
    depth = w_in.shape[0]
    n_keep = min(TOPK_MAX, seq // 4)
    t = seq + N_META
    tp = -(-t // ATT_TILE) * ATT_TILE
    rows = b * tp

    h = jnp.concatenate([jnp.broadcast_to(meta_tokens.astype(x.dtype)[None], (b, N_META, d)), x,
                         jnp.zeros((b, tp - t, d), x.dtype)], axis=1).reshape(rows, d)
    tabs64, tabs128 = _rope_tables(tp)
    ret_consts = _retention_consts(ATT_TILE)
    row = lambda v: v.astype(f32)[None, :]
    twice = lambda v: jnp.concatenate([v, v]).astype(f32)[None, :]

    for l in range(depth):
        lambda_init = 0.8 - 0.6 * math.exp(-0.3 * l)
        h = _ffn(h, row(ffn1_norm[l]), ffn1_w_in[l].astype(bf16), ffn1_w_out[l].astype(bf16))
        proj = _proj(h, row(mix_norm[l]), _permute_w_in(w_in[l]).astype(bf16)).reshape(b, tp, NP)

        qz, ka, va = _prep_a(proj, tabs64, twice(diff_q_norm[l]), twice(diff_k_norm[l]))
        y_a = _attn_a(qz, ka, va, row(diff_lambda_q1[l]), row(diff_lambda_k1[l]),
                      row(diff_lambda_q2[l]), row(diff_lambda_k2[l]), row(diff_out_norm[l]),
                      lambda_init)
        y_b = _gdn(proj, gdn_conv_w[l].astype(f32), _lane_row(gdn_a_log[l], SM_ALPHA),
                   _lane_row(gdn_dt_bias[l], SM_ALPHA), row(gdn_out_norm[l]))
        y_c = _retention(proj, tabs64, ret_consts, row(ret_out_norm[l]))
        qd, kd, vd, iq, ik = _prep_d(proj, tabs64, tabs128, row(dsa_q_norm[l]), row(dsa_k_norm[l]))
        y_d = _dsa(iq, proj, ik, qd, kd, vd, n_keep)

        ys = [y.reshape(rows, GROUP) for y in (y_a, y_b, y_c, y_d)]
        h = _outproj(h, ys, w_out[l].astype(bf16))
        h = _ffn(h, row(ffn2_norm[l]), ffn2_w_in[l].astype(bf16), ffn2_w_out[l].astype(bf16))
    return h.reshape(b, tp, d)[:, N_META:t]
```

```python
import functools
import math

import jax
import jax.numpy as jnp
import numpy as np
from jax import lax
from jax.experimental import pallas as pl
from jax.experimental.pallas import tpu as pltpu

f32 = jnp.float32
bf16 = jnp.bfloat16
HIGHEST = lax.Precision.HIGHEST

N_META = 16
HEAD_DIM = 128
HEADS = 4
GROUP = HEADS * HEAD_DIM
EPS = 1e-6
ROPE_THETA = 10000.0
NEG = -1e30
TOPK_MAX = 256
CONV_K = 4
IDX_HEADS = 16
IDX_DIM = 64
CHUNK = 64
LANES = 128
ATT_TILE = 384
VMEM_LIMIT = 56 * 1024 * 1024

NP = 8192
COL_A = 0
COL_B = 1536
COL_C = 3584
COL_D = 5120
COL_IK = 7680
COL_SM = 7808
SM_IW, SM_BETA, SM_ALPHA = 0, 16, 20

_NT = (((1,), (1,)), ((), ()))


def _cparams(sem):
    return pltpu.CompilerParams(dimension_semantics=sem, vmem_limit_bytes=VMEM_LIMIT)


def _row_tile(rows, target):
    best = 8
    for t in range(8, target + 1, 8):
        if rows % t == 0:
            best = t
    return best


def _silu(x):
    return x * jax.nn.sigmoid(x)


def _rms(x, g):
    return x * lax.rsqrt(jnp.mean(x * x, axis=-1, keepdims=True) + EPS) * g


def _ffn_kernel(h_ref, g_ref, wg_ref, wu_ref, wo_ref, o_ref, xn_ref):
    @pl.when(pl.program_id(1) == 0)
    def _():
        x = h_ref[...]
        xn_ref[...] = _rms(x, g_ref[...]).astype(bf16)
        o_ref[...] = x

    xn = xn_ref[...]
    gate = jnp.dot(xn, wg_ref[...], preferred_element_type=f32)
    up = jnp.dot(xn, wu_ref[...], preferred_element_type=f32)
    act = (_silu(gate) * up).astype(bf16)
    o_ref[...] += 0.5 * jnp.dot(act, wo_ref[...], preferred_element_type=f32)


def _ffn(h, g, w_in, w_out, l):
    rows, d = h.shape
    ff = w_out.shape[1]
    tm = _row_tile(rows, 768)
    tf = 512
    nf = ff // tf
    return pl.pallas_call(
        _ffn_kernel,
        grid=(rows // tm, nf),
        in_specs=[
            pl.BlockSpec((tm, d), lambda i, j: (i, 0)),
            pl.BlockSpec((None, 1, d), lambda i, j: (l, 0, 0)),
            pl.BlockSpec((None, d, tf), lambda i, j: (l, 0, j)),
            pl.BlockSpec((None, d, tf), lambda i, j: (l, 0, j + nf)),
            pl.BlockSpec((None, tf, d), lambda i, j: (l, j, 0)),
        ],
        out_specs=pl.BlockSpec((tm, d), lambda i, j: (i, 0)),
        out_shape=jax.ShapeDtypeStruct((rows, d), f32),
        scratch_shapes=[pltpu.VMEM((tm, d), bf16)],
        compiler_params=_cparams(("parallel", "arbitrary")),
        name="ffn",
    )(h, g, w_in, w_in, w_out)


def _proj_kernel(h_ref, g_ref, w_ref, o_ref, xn_ref):
    @pl.when(pl.program_id(1) == 0)
    def _():
        xn_ref[...] = _rms(h_ref[...], g_ref[...]).astype(bf16)

    o_ref[...] = jnp.dot(xn_ref[...], w_ref[...], preferred_element_type=f32)


def _proj(h, g, w, l):
    rows, d = h.shape
    n = w.shape[2]
    tm = _row_tile(rows, 768)
    tn = 1024
    return pl.pallas_call(
        _proj_kernel,
        grid=(rows // tm, n // tn),
        in_specs=[
            pl.BlockSpec((tm, d), lambda i, j: (i, 0)),
            pl.BlockSpec((None, 1, d), lambda i, j: (l, 0, 0)),
            pl.BlockSpec((None, d, tn), lambda i, j: (l, 0, j)),
        ],
        out_specs=pl.BlockSpec((tm, tn), lambda i, j: (i, j)),
        out_shape=jax.ShapeDtypeStruct((rows, n), f32),
        scratch_shapes=[pltpu.VMEM((tm, d), bf16)],
        compiler_params=_cparams(("parallel", "arbitrary")),
        name="proj",
    )(h, g, w)


def _outproj_kernel(h_ref, ya_ref, yb_ref, yc_ref, yd_ref, w_ref, o_ref):
    acc = h_ref[...]
    for n, y_ref in enumerate((ya_ref, yb_ref, yc_ref, yd_ref)):
        acc = acc + jnp.dot(y_ref[...], w_ref[n * GROUP:(n + 1) * GROUP, :],
                            preferred_element_type=f32)
    o_ref[...] = acc


def _outproj(h, ys, w, l):
    rows, d = h.shape
    tm = _row_tile(rows, 768)
    y_spec = pl.BlockSpec((tm, GROUP), lambda i: (i, 0))
    return pl.pallas_call(
        _outproj_kernel,
        grid=(rows // tm,),
        in_specs=[pl.BlockSpec((tm, d), lambda i: (i, 0)), y_spec, y_spec, y_spec, y_spec,
                  pl.BlockSpec((None, w.shape[1], d), lambda i: (l, 0, 0))],
        out_specs=pl.BlockSpec((tm, d), lambda i: (i, 0)),
        out_shape=jax.ShapeDtypeStruct((rows, d), f32),
        compiler_params=_cparams(("parallel",)),
        name="outproj",
    )(h, *ys, w)


def _rope64(x, c, sa, sb):
    return x * c + pltpu.roll(x, LANES - 32, 1) * sa + pltpu.roll(x, 32, 1) * sb


def _rope128(x, c, s):
    return x * c + pltpu.roll(x, 64, 1) * s


def _group_mean_matrix():
    r = lax.broadcasted_iota(jnp.int32, (LANES, LANES), 0) >> 6
    c = lax.broadcasted_iota(jnp.int32, (LANES, LANES), 1) >> 6
    return jnp.where(r == c, 1.0 / 64.0, 0.0).astype(f32)


def _prep_a_kernel(q_ref, k_ref, v_ref, c_ref, sa_ref, sb_ref, gq_ref, gk_ref,
                   qz_ref, ko_ref, vo_ref):
    tm = q_ref.shape[0]
    lane = lax.broadcasted_iota(jnp.int32, (tm, LANES), 1)
    gmat = _group_mean_matrix()
    c, sa, sb = c_ref[...], sa_ref[...], sb_ref[...]

    def norm_rope(x, g):
        ms = jnp.dot(x * x, gmat, precision=HIGHEST, preferred_element_type=f32)
        return _rope64(x * lax.rsqrt(ms + EPS) * g, c, sa, sb)

    for h in range(HEADS):
        sl = slice(h * HEAD_DIM, (h + 1) * HEAD_DIM)
        q = norm_rope(q_ref[:, sl], gq_ref[...]) * (64 ** -0.5)
        qz_ref[h, 0] = jnp.where(lane < 64, q, 0.0).astype(bf16)
        qz_ref[h, 1] = jnp.where(lane >= 64, q, 0.0).astype(bf16)
        ko_ref[h] = norm_rope(k_ref[:, sl], gk_ref[...]).astype(bf16)
        vo_ref[h] = v_ref[:, sl].astype(bf16)


def _prep_a(proj, tabs64, gq, gk):
    b, tp, _ = proj.shape
    tm = ATT_TILE
    cb = COL_A // GROUP
    tab_spec = pl.BlockSpec((tm, LANES), lambda bi, i: (i, 0))
    g_spec = pl.BlockSpec((1, LANES), lambda bi, i: (0, 0))
    hspec = pl.BlockSpec((None, HEADS, tm, HEAD_DIM), lambda bi, i: (bi, 0, i, 0))
    return pl.pallas_call(
        _prep_a_kernel,
        grid=(b, tp // tm),
        in_specs=[
            pl.BlockSpec((None, tm, GROUP), lambda bi, i: (bi, i, cb)),
            pl.BlockSpec((None, tm, GROUP), lambda bi, i: (bi, i, cb + 1)),
            pl.BlockSpec((None, tm, GROUP), lambda bi, i: (bi, i, cb + 2)),
            tab_spec, tab_spec, tab_spec, g_spec, g_spec,
        ],
        out_specs=[
            pl.BlockSpec((None, HEADS, 2, tm, HEAD_DIM), lambda bi, i: (bi, 0, 0, i, 0)),
            hspec, hspec,
        ],
        out_shape=[
            jax.ShapeDtypeStruct((b, HEADS, 2, tp, HEAD_DIM), bf16),
            jax.ShapeDtypeStruct((b, HEADS, tp, HEAD_DIM), bf16),
            jax.ShapeDtypeStruct((b, HEADS, tp, HEAD_DIM), bf16),
        ],
        compiler_params=_cparams(("parallel", "parallel")),
        name="prep_a",
    )(proj, proj, proj, *tabs64, gq, gk)


def _attn_a_kernel(qz_ref, k_ref, v_ref, lq1_ref, lk1_ref, lq2_ref, lk2_ref, gn_ref, o_ref,
                   *, tq, lambda_init):
    i = pl.program_id(2)
    q = qz_ref[...].reshape(2 * tq, HEAD_DIM)
    row = lax.broadcasted_iota(jnp.int32, (2 * tq, tq), 0)
    col = lax.broadcasted_iota(jnp.int32, (2 * tq, tq), 1)
    causal = col <= jnp.where(row >= tq, row - tq, row)

    def step(j, carry, masked):
        m, l, acc = carry
        start = pl.multiple_of(j * tq, tq)
        kb = k_ref[pl.ds(start, tq), :]
        vb = v_ref[pl.ds(start, tq), :]
        s = lax.dot_general(q, kb, _NT, preferred_element_type=f32)
        if masked:
            s = jnp.where(causal, s, NEG)
        m_new = jnp.maximum(m, jnp.max(s, axis=-1, keepdims=True))
        alpha = jnp.exp(m - m_new)
        p = jnp.exp(s - m_new)
        l = alpha * l + jnp.sum(p, axis=-1, keepdims=True)
        acc = alpha * acc + jnp.dot(p.astype(bf16), vb, preferred_element_type=f32)
        return m_new, l, acc

    init = (jnp.full((2 * tq, 1), NEG, f32), jnp.zeros((2 * tq, 1), f32),
            jnp.zeros((2 * tq, HEAD_DIM), f32))
    carry = lax.fori_loop(0, i, lambda j, c: step(j, c, False), init)
    m, l, acc = step(i, carry, True)
    o = acc / l
    lam = (jnp.exp(jnp.sum(lq1_ref[...] * lk1_ref[...], axis=-1, keepdims=True))
           - jnp.exp(jnp.sum(lq2_ref[...] * lk2_ref[...], axis=-1, keepdims=True))
           + lambda_init)
    od = o[:tq] - lam * o[tq:]
    o_ref[...] = (_rms(od, gn_ref[...]) * (1.0 - lambda_init)).astype(bf16)


def _attn_a(qz, k, v, lq1, lk1, lq2, lk2, gn, lambda_init):
    b, _, _, tp, _ = qz.shape
    tq = ATT_TILE
    kv_spec = pl.BlockSpec((None, None, tp, HEAD_DIM), lambda bi, h, i: (bi, h, 0, 0))
    l_spec = pl.BlockSpec((1, 64), lambda bi, h, i: (0, 0))
    return pl.pallas_call(
        functools.partial(_attn_a_kernel, tq=tq, lambda_init=lambda_init),
        grid=(b, HEADS, tp // tq),
        in_specs=[
            pl.BlockSpec((None, None, 2, tq, HEAD_DIM), lambda bi, h, i: (bi, h, 0, i, 0)),
            kv_spec, kv_spec, l_spec, l_spec, l_spec, l_spec,
            pl.BlockSpec((1, HEAD_DIM), lambda bi, h, i: (0, 0)),
        ],
        out_specs=pl.BlockSpec((None, tq, HEAD_DIM), lambda bi, h, i: (bi, i, h)),
        out_shape=jax.ShapeDtypeStruct((b, tp, GROUP), bf16),
        compiler_params=_cparams(("parallel", "parallel", "arbitrary")),
        name="attn_a",
    )(qz, k, v, lq1, lk1, lq2, lk2, gn)


def _gdn_kernel(qkv_ref, z_ref, sm_ref, cw_ref, alog_ref, dtb_ref, gn_ref, o_ref,
                xx_ref, s_ref, *, tb):
    w3 = 3 * GROUP

    @pl.when(pl.program_id(1) == 0)
    def _():
        xx_ref[0:8, :] = jnp.zeros((8, w3), f32)
        s_ref[...] = jnp.zeros_like(s_ref)

    x = qkv_ref[...]
    xx_ref[8:tb + 8, :] = x
    cw = cw_ref[...]
    y = x * cw[CONV_K - 1:CONV_K, :]
    for s in range(1, CONV_K):
        y = y + xx_ref[pl.ds(8 - s, tb), :] * cw[CONV_K - 1 - s:CONV_K - s, :]
    xx_ref[0:8, :] = x[tb - 8:, :]
    y = _silu(y)

    sm = sm_ref[...]
    beta_all = jax.nn.sigmoid(sm)
    g_all = -jnp.exp(alog_ref[...]) * jax.nn.softplus(sm + dtb_ref[...])

    r = HEADS * CHUNK
    ri = lax.broadcasted_iota(jnp.int32, (r, r), 0)
    ci = lax.broadcasted_iota(jnp.int32, (r, r), 1)
    same = (ri >> 6) == (ci >> 6)
    tri = same & (ri >= ci)
    strict = same & (ri > ci)
    lmat = jnp.where(tri, 1.0, 0.0).astype(bf16)
    eye = jnp.where(ri == ci, 1.0, 0.0).astype(f32)
    own = ((lax.broadcasted_iota(jnp.int32, (r, GROUP), 0) >> 6)
           == (lax.broadcasted_iota(jnp.int32, (r, GROUP), 1) >> 7))
    gn = gn_ref[...]

    def bdot(a, b):
        return jnp.dot(a, b, preferred_element_type=f32)

    def split2(x):
        hi = x.astype(bf16)
        return hi, (x - hi.astype(f32)).astype(bf16)

    def split3(x):
        hi = x.astype(bf16)
        r1 = x - hi.astype(f32)
        mid = r1.astype(bf16)
        return hi, mid, (r1 - mid.astype(f32)).astype(bf16)

    def stack(fn):
        return jnp.concatenate([fn(h) for h in range(HEADS)], axis=0)

    def diag_blocks(x):
        return stack(lambda h: x[h * CHUNK:(h + 1) * CHUNK, h * HEAD_DIM:(h + 1) * HEAD_DIM])

    nc = tb // CHUNK
    chunks = range(nc)
    rows = [slice(c * CHUNK, (c + 1) * CHUNK) for c in chunks]

    def heads_of(c, base):
        return stack(lambda h: y[rows[c], base + h * HEAD_DIM:base + (h + 1) * HEAD_DIM])

    def gate_of(c, vals, lane0):
        return stack(lambda h: jnp.broadcast_to(vals[rows[c], lane0 + h:lane0 + h + 1],
                                                (CHUNK, LANES)))

    qs = [heads_of(c, 0) for c in chunks]
    ks = [heads_of(c, GROUP) for c in chunks]
    vs = [heads_of(c, 2 * GROUP) for c in chunks]
    qs = [q * lax.rsqrt(jnp.sum(q * q, axis=-1, keepdims=True) + EPS) * (HEAD_DIM ** -0.5)
          for q in qs]
    ks = [k * lax.rsqrt(jnp.sum(k * k, axis=-1, keepdims=True) + EPS) for k in ks]
    bs = [gate_of(c, beta_all, SM_BETA) for c in chunks]
    gs = [gate_of(c, g_all, SM_ALPHA) for c in chunks]
    gy = [bdot(lmat, jnp.concatenate(
        split3(jnp.where(strict, jnp.concatenate([g, g], axis=1), 0.0)) + split3(g), axis=1))
        for g in gs]
    diff = [(t[:, 2 * r:3 * r] + t[:, r:2 * r]) + t[:, :r] for t in gy]
    gcum = [(t[:, 3 * r + 2 * LANES:] + t[:, 3 * r + LANES:3 * r + 2 * LANES])
            + t[:, 3 * r:3 * r + LANES] for t in gy]
    decay = [jnp.where(tri, jnp.exp(t), 0.0) for t in diff]
    egc = [jnp.exp(t) for t in gcum]
    kb = [k * b_ for k, b_ in zip(ks, bs)]
    ksb = [k.astype(bf16) for k in ks]
    a_mat = [jnp.where(strict, lax.dot_general(kb[c].astype(bf16), ksb[c], _NT,
                                               preferred_element_type=f32) * decay[c], 0.0)
             for c in chunks]
    t_inv = [eye - a for a in a_mat]
    a_sp = [split2(a) for a in a_mat]
    r1 = [bdot(jnp.concatenate([a_h, a_l], axis=0), a_h) for a_h, a_l in a_sp]
    r2 = [bdot(a_h, a_l) for a_h, a_l in a_sp]
    pw = [(r1[c][r:] + r2[c]) + r1[c][:r] for c in chunks]
    n_sq = int(math.log2(CHUNK)) - 1
    for it in range(n_sq):
        p_sp = [split2(p) for p in pw]
        t_sp = [split2(t) for t in t_inv]
        if it + 1 < n_sq:
            r1 = [bdot(jnp.concatenate([t_sp[c][0], p_sp[c][0], t_sp[c][1], p_sp[c][1]], axis=0),
                       p_sp[c][0]) for c in chunks]
            r2 = [bdot(jnp.concatenate([t_sp[c][0], p_sp[c][0]], axis=0), p_sp[c][1])
                  for c in chunks]
            t_inv = [t_inv[c] + ((r1[c][2 * r:3 * r] + r2[c][:r]) + r1[c][:r]) for c in chunks]
            pw = [(r1[c][3 * r:] + r2[c][r:]) + r1[c][r:2 * r] for c in chunks]
        else:
            r1 = [bdot(jnp.concatenate([t_sp[c][0], t_sp[c][1]], axis=0), p_sp[c][0])
                  for c in chunks]
            r2 = [bdot(t_sp[c][0], p_sp[c][1]) for c in chunks]
            t_inv = [t_inv[c] + ((r1[c][r:] + r2[c]) + r1[c][:r]) for c in chunks]
    uw = [bdot(t_inv[c].astype(bf16),
               jnp.concatenate([vs[c] * bs[c], kb[c] * egc[c]], axis=1).astype(bf16))
          for c in chunks]
    attn = [(lax.dot_general(qs[c].astype(bf16), ksb[c], _NT, preferred_element_type=f32)
             * decay[c]).astype(bf16) for c in chunks]
    qg = [(qs[c] * egc[c]).astype(bf16) for c in chunks]
    last = lambda t, h: t[(h + 1) * CHUNK - 1:(h + 1) * CHUNK, :]
    kd = [(ks[c] * jnp.exp(stack(lambda h: jnp.broadcast_to(last(gcum[c], h), (CHUNK, LANES)))
                           - gcum[c])).T.astype(bf16) for c in chunks]
    elast = [jnp.concatenate([jnp.exp(last(gcum[c], h)) for h in range(HEADS)], axis=1)
             for c in chunks]

    state = s_ref[...]
    for c in chunks:
        u, w = uw[c][:, :HEAD_DIM], uw[c][:, HEAD_DIM:]
        sb = state.astype(bf16)
        v_new = u - diag_blocks(bdot(w.astype(bf16), sb))
        o = diag_blocks(bdot(qg[c], sb)) + bdot(attn[c], v_new.astype(bf16))
        v_bd = jnp.where(own, jnp.concatenate([v_new] * HEADS, axis=1), 0.0).astype(bf16)
        state = state * elast[c] + bdot(kd[c], v_bd)
        for h in range(HEADS):
            sl = slice(h * HEAD_DIM, (h + 1) * HEAD_DIM)
            zh = z_ref[rows[c], sl]
            o_ref[rows[c], sl] = (_rms(o[h * CHUNK:(h + 1) * CHUNK], gn) * _silu(zh)).astype(bf16)
    s_ref[...] = state


def _gdn(proj, conv_w, alog_row, dtb_row, gn):
    b, tp, _ = proj.shape
    tb = ATT_TILE
    w3 = 3 * GROUP
    return pl.pallas_call(
        functools.partial(_gdn_kernel, tb=tb),
        grid=(b, tp // tb),
        in_specs=[
            pl.BlockSpec((None, tb, w3), lambda bi, t: (bi, t, COL_B // w3)),
            pl.BlockSpec((None, tb, GROUP), lambda bi, t: (bi, t, (COL_B + w3) // GROUP)),
            pl.BlockSpec((None, tb, LANES), lambda bi, t: (bi, t, COL_SM // LANES)),
            pl.BlockSpec((CONV_K, w3), lambda bi, t: (0, 0)),
            pl.BlockSpec((1, LANES), lambda bi, t: (0, 0)),
            pl.BlockSpec((1, LANES), lambda bi, t: (0, 0)),
            pl.BlockSpec((1, HEAD_DIM), lambda bi, t: (0, 0)),
        ],
        out_specs=pl.BlockSpec((None, tb, GROUP), lambda bi, t: (bi, t, 0)),
        out_shape=jax.ShapeDtypeStruct((b, tp, GROUP), bf16),
        scratch_shapes=[pltpu.VMEM((tb + 8, w3), f32), pltpu.VMEM((HEAD_DIM, GROUP), f32)],
        compiler_params=_cparams(("parallel", "arbitrary")),
        name="gdn",
    )(proj, proj, proj, conv_w, alog_row, dtb_row, gn)


def _ret_kernel(qk_ref, v_ref, g_ref, c_ref, sa_ref, sb_ref, dm_ref, xi_ref, zt_ref, gch_ref,
                gn_ref, o_ref, r_ref, *, tc):
    @pl.when(pl.program_id(1) == 0)
    def _():
        r_ref[...] = jnp.zeros_like(r_ref)

    lane = lax.broadcasted_iota(jnp.int32, (tc, LANES), 1)
    c, sa, sb = c_ref[...], sa_ref[...], sb_ref[...]
    gn = gn_ref[...]
    for pair in range(HEADS // 2):
        qp = _rope64(qk_ref[:, pair * LANES:(pair + 1) * LANES], c, sa, sb)
        kp = _rope64(qk_ref[:, 2 * LANES + pair * LANES:2 * LANES + (pair + 1) * LANES],
                     c, sa, sb) * (64 ** -0.5)
        for half in range(2):
            h = 2 * pair + half
            sl = slice(h * HEAD_DIM, (h + 1) * HEAD_DIM)
            own = (lane >= 64) if half else (lane < 64)
            qm = jnp.where(own, qp, 0.0)
            km = jnp.where(own, kp, 0.0)
            vb = v_ref[:, sl].astype(bf16)
            inner = lax.dot_general(qm.astype(bf16), km.astype(bf16), _NT,
                                    preferred_element_type=f32) * dm_ref[h]
            r = r_ref[h]
            o = (jnp.dot(inner.astype(bf16), vb, preferred_element_type=f32)
                 + jnp.dot((qm * xi_ref[h]).astype(bf16), r.astype(bf16),
                           preferred_element_type=f32))
            kz = (km * zt_ref[h]).T.astype(bf16)
            r_ref[h] = r * gch_ref[h] + jnp.dot(kz, vb, preferred_element_type=f32)
            o_ref[:, sl] = (_silu(g_ref[:, sl]) * _rms(o, gn)).astype(bf16)


def _retention(proj, tabs64, consts, gn):
    b, tp, _ = proj.shape
    tc = ATT_TILE
    dmat, xi, zeta, gch = consts
    tab_spec = pl.BlockSpec((tc, LANES), lambda bi, t: (t, 0))
    cb = COL_C // GROUP
    full3 = lambda a: pl.BlockSpec(a.shape, lambda bi, t: (0, 0, 0))
    return pl.pallas_call(
        functools.partial(_ret_kernel, tc=tc),
        grid=(b, tp // tc),
        in_specs=[
            pl.BlockSpec((None, tc, GROUP), lambda bi, t: (bi, t, cb)),
            pl.BlockSpec((None, tc, GROUP), lambda bi, t: (bi, t, cb + 1)),
            pl.BlockSpec((None, tc, GROUP), lambda bi, t: (bi, t, cb + 2)),
            tab_spec, tab_spec, tab_spec,
            full3(dmat), full3(xi), full3(zeta), full3(gch),
            pl.BlockSpec((1, HEAD_DIM), lambda bi, t: (0, 0)),
        ],
        out_specs=pl.BlockSpec((None, tc, GROUP), lambda bi, t: (bi, t, 0)),
        out_shape=jax.ShapeDtypeStruct((b, tp, GROUP), bf16),
        scratch_shapes=[pltpu.VMEM((HEADS, HEAD_DIM, HEAD_DIM), f32)],
        compiler_params=_cparams(("parallel", "arbitrary")),
        name="retention",
    )(proj, proj, proj, *tabs64, dmat, xi, zeta, gch, gn)


def _retention_consts(tc):
    log_g = jnp.log(1.0 - 2.0 ** (-5.0 - jnp.arange(HEADS, dtype=f32)))
    i = jnp.arange(tc, dtype=f32)
    dist = i[:, None] - i[None, :]
    dmat = jnp.where(dist >= 0, jnp.exp(log_g[:, None, None] * jnp.maximum(dist, 0.0)), 0.0)
    ones = jnp.ones((1, 1, LANES), f32)
    xi = jnp.exp(log_g[:, None] * (i + 1.0))[..., None] * ones
    zeta = jnp.exp(log_g[:, None] * (tc - 1.0 - i))[..., None] * ones
    gch = jnp.exp(log_g * tc)[:, None, None] * ones
    return dmat, xi, zeta, gch


def _prep_d_kernel(q_ref, k_ref, v_ref, iq0_ref, iq1_ref, ik_ref, c64_ref, sa_ref, sb_ref,
                   c128_ref, s128_ref, gq_ref, gk_ref,
                   qo_ref, ko_ref, vo_ref, iqo_ref, iko_ref):
    c64, sa, sb = c64_ref[...], sa_ref[...], sb_ref[...]
    c128, s128 = c128_ref[...], s128_ref[...]
    for h in range(HEADS):
        sl = slice(h * HEAD_DIM, (h + 1) * HEAD_DIM)
        q = _rope128(_rms(q_ref[:, sl], gq_ref[...]), c128, s128) * (HEAD_DIM ** -0.5)
        qo_ref[h] = q.astype(bf16)
        ko_ref[h] = _rope128(_rms(k_ref[:, sl], gk_ref[...]), c128, s128).astype(bf16)
        vo_ref[h] = v_ref[:, sl].astype(bf16)
    for p in range(IDX_HEADS // 2):
        src = iq0_ref if p < 4 else iq1_ref
        x = _rope64(src[:, (p % 4) * LANES:(p % 4 + 1) * LANES], c64, sa, sb)
        iqo_ref[2 * p] = x[:, :IDX_DIM].astype(bf16)
        iqo_ref[2 * p + 1] = x[:, IDX_DIM:].astype(bf16)
    iko_ref[...] = _rope64(ik_ref[...], c64, sa, sb)[:, :IDX_DIM].astype(bf16)


def _prep_d(proj, tabs64, tabs128, gq, gk):
    b, tp, _ = proj.shape
    tm = ATT_TILE
    cb = COL_D // GROUP
    tab_spec = pl.BlockSpec((tm, LANES), lambda bi, i: (i, 0))
    g_spec = pl.BlockSpec((1, LANES), lambda bi, i: (0, 0))
    col = lambda k: pl.BlockSpec((None, tm, GROUP), lambda bi, i: (bi, i, cb + k))
    hspec = pl.BlockSpec((None, HEADS, tm, HEAD_DIM), lambda bi, i: (bi, 0, i, 0))
    hshape = jax.ShapeDtypeStruct((b, HEADS, tp, HEAD_DIM), bf16)
    return pl.pallas_call(
        _prep_d_kernel,
        grid=(b, tp // tm),
        in_specs=[col(0), col(1), col(2), col(3), col(4),
                  pl.BlockSpec((None, tm, LANES), lambda bi, i: (bi, i, COL_IK // LANES)),
                  tab_spec, tab_spec, tab_spec, tab_spec, tab_spec, g_spec, g_spec],
        out_specs=[hspec, hspec, hspec,
                   pl.BlockSpec((None, IDX_HEADS, tm, IDX_DIM), lambda bi, i: (bi, 0, i, 0)),
                   pl.BlockSpec((None, tm, IDX_DIM), lambda bi, i: (bi, i, 0))],
        out_shape=[hshape, hshape, hshape,
                   jax.ShapeDtypeStruct((b, IDX_HEADS, tp, IDX_DIM), bf16),
                   jax.ShapeDtypeStruct((b, tp, IDX_DIM), bf16)],
        compiler_params=_cparams(("parallel", "parallel")),
        name="prep_d",
    )(proj, proj, proj, proj, proj, proj, *tabs64, *tabs128, gq, gk)


def _key_to_f32(key):
    return lax.bitcast_convert_type(jnp.where(key < 0, key ^ 0x7FFFFFFF, key), f32)


def _dsa_kernel(iq_ref, iw_ref, ik_ref, q_ref, k_ref, v_ref, o_ref, sc_ref, cs_ref,
                *, tq, n_keep):
    i = pl.program_id(1)
    nkb = i + 1
    row = lax.broadcasted_iota(jnp.int32, (tq, tq), 0)
    col = lax.broadcasted_iota(jnp.int32, (tq, tq), 1)
    causal = col <= row
    iw = iw_ref[...]
    kf = float(n_keep)
    int_min = -2 ** 31
    neg_inf_key = int_min + 0x7FFFFF

    def scores(j, masked):
        start = pl.multiple_of(j * tq, tq)
        ikb = ik_ref[pl.ds(start, tq), :]
        sc = jnp.zeros((tq, tq), f32)
        for hh in range(IDX_HEADS):
            s = lax.dot_general(iq_ref[hh], ikb, _NT, preferred_element_type=f32)
            sc = sc + iw[:, SM_IW + hh:SM_IW + hh + 1] * jnp.maximum(s, 0.0)
        if masked:
            sc = jnp.where(causal, sc, -jnp.inf)
        sc_ref[j] = sc

    def _scores_body(j, carry):
        scores(j, False)
        return carry

    lax.fori_loop(0, i, _scores_body, 0)
    scores(i, True)

    def lane_fold(x):
        out = x[:, :LANES]
        for t in range(1, tq // LANES):
            out = out + x[:, t * LANES:(t + 1) * LANES]
        return out

    def count(pred):
        def body(j, acc):
            return acc + lane_fold(jnp.where(pred(sc_ref[j], j), 1.0, 0.0))
        acc = lax.fori_loop(0, nkb, body, jnp.zeros((tq, LANES), f32))
        return jnp.sum(acc, axis=-1, keepdims=True)

    def bisect(it, ans):
        cand = ans + lax.shift_left(jnp.int32(1), 31 - it)
        cand_f = _key_to_f32(cand)
        cnt = count(lambda blk, j: blk >= cand_f)
        return jnp.where(cnt >= kf, cand, ans)

    ans = lax.fori_loop(0, 32, bisect, jnp.full((tq, 1), int_min, jnp.int32))
    thr_key = jnp.maximum(ans, neg_inf_key)
    thr = _key_to_f32(thr_key)

    n_ge = count(lambda blk, j: blk >= thr)
    n_gt = count(lambda blk, j: blk > thr)
    need = (n_ge > kf) & (thr_key > neg_inf_key)
    big = float(2 ** 30)
    cs_ref[...] = jnp.full((tq, LANES), big, f32)

    @pl.when(jnp.max(jnp.where(need, 1.0, 0.0)) > 0.0)
    def _():
        budget = kf - n_gt

        def tie_bisect(it, lo):
            cand = lo + lax.shift_left(jnp.int32(1), 14 - it).astype(f32)

            def pred(blk, j):
                colg = (col + j * tq).astype(f32)
                return jnp.where(blk == thr, colg, big) < cand
            cnt = count(pred)
            return jnp.where(cnt <= budget - 1.0, cand, lo)

        cstar = lax.fori_loop(0, 15, tie_bisect, jnp.zeros((tq, 1), f32))
        cs_ref[...] = jnp.broadcast_to(jnp.where(need, cstar, big), (tq, LANES))

    cstar = cs_ref[:, 0:1]

    def attend(j, carry, masked):
        start = pl.multiple_of(j * tq, tq)
        blk = sc_ref[j]
        colg = (col + j * tq).astype(f32)
        sel = (blk > thr) | ((blk == thr) & (colg <= cstar))
        if masked:
            sel = sel & causal
        out = []
        for h in range(HEADS):
            m, l, acc = carry[h]
            kb = k_ref[h, pl.ds(start, tq), :]
            vb = v_ref[h, pl.ds(start, tq), :]
            s = lax.dot_general(q_ref[h], kb, _NT, preferred_element_type=f32)
            s = jnp.where(sel, s, NEG)
            m_new = jnp.maximum(m, jnp.max(s, axis=-1, keepdims=True))
            alpha = jnp.exp(m - m_new)
            p = jnp.where(sel, jnp.exp(s - m_new), 0.0)
            l = alpha * l + jnp.sum(p, axis=-1, keepdims=True)
            acc = alpha * acc + jnp.dot(p.astype(bf16), vb, preferred_element_type=f32)
            out.append((m_new, l, acc))
        return tuple(out)

    init = tuple((jnp.full((tq, 1), NEG, f32), jnp.zeros((tq, 1), f32),
                  jnp.zeros((tq, HEAD_DIM), f32)) for _ in range(HEADS))
    carry = lax.fori_loop(0, i, lambda j, c: attend(j, c, False), init)
    carry = attend(i, carry, True)
    for h in range(HEADS):
        _, l, acc = carry[h]
        o_ref[:, h * HEAD_DIM:(h + 1) * HEAD_DIM] = (acc / l).astype(bf16)


def _dsa(iq, proj, ik, q, k, v, n_keep):
    b, _, tp, _ = q.shape
    tq = ATT_TILE
    assert n_keep <= tq
    nq = tp // tq
    kv_spec = pl.BlockSpec((None, HEADS, tp, HEAD_DIM), lambda bi, i: (bi, 0, 0, 0))
    return pl.pallas_call(
        functools.partial(_dsa_kernel, tq=tq, n_keep=n_keep),
        grid=(b, nq),
        in_specs=[
            pl.BlockSpec((None, IDX_HEADS, tq, IDX_DIM), lambda bi, i: (bi, 0, i, 0)),
            pl.BlockSpec((None, tq, LANES), lambda bi, i: (bi, i, COL_SM // LANES)),
            pl.BlockSpec((None, tp, IDX_DIM), lambda bi, i: (bi, 0, 0)),
            pl.BlockSpec((None, HEADS, tq, HEAD_DIM), lambda bi, i: (bi, 0, i, 0)),
            kv_spec, kv_spec,
        ],
        out_specs=pl.BlockSpec((None, tq, GROUP), lambda bi, i: (bi, i, 0)),
        out_shape=jax.ShapeDtypeStruct((b, tp, GROUP), bf16),
        scratch_shapes=[pltpu.VMEM((nq, tq, tq), f32), pltpu.VMEM((tq, LANES), f32)],
        compiler_params=_cparams(("parallel", "arbitrary")),
        name="dsa",
    )(iq, proj, ik, q, k, v)


def _rope_tables(tp):
    pos = jnp.arange(tp, dtype=f32)

    def cs(d):
        inv = ROPE_THETA ** (-jnp.arange(0, d, 2, dtype=f32) / d)
        ang = pos[:, None] * inv[None, :]
        return jnp.cos(ang), jnp.sin(ang)

    c, s = cs(64)
    z = jnp.zeros_like(s)
    tabs64 = (jnp.tile(jnp.concatenate([c, c], axis=1), (1, 2)),
              jnp.tile(jnp.concatenate([-s, z], axis=1), (1, 2)),
              jnp.tile(jnp.concatenate([z, s], axis=1), (1, 2)))
    c, s = cs(128)
    tabs128 = (jnp.concatenate([c, c], axis=1), jnp.concatenate([-s, s], axis=1))
    return tabs64, tabs128


def _permute_w_in(w):
    z = lambda n: jnp.zeros(w.shape[:-1] + (n,), w.dtype)
    o_beta = 7 * GROUP
    o_c = o_beta + 2 * HEADS
    o_ik = o_c + 8 * GROUP
    o_iw = o_ik + IDX_DIM
    end = o_iw + IDX_HEADS
    return jnp.concatenate([
        w[..., :o_beta], w[..., o_c:o_ik], w[..., o_ik:o_iw], z(LANES - IDX_DIM),
        w[..., o_iw:end], w[..., o_beta:o_c], z(NP - COL_SM - IDX_HEADS - 2 * HEADS)], axis=-1)


def _lane_row(v, offset):
    return jnp.zeros((1, LANES), f32).at[0, offset:offset + v.shape[0]].set(v.astype(f32))


def kernel(x, meta_tokens, ffn1_norm, ffn1_w_in, ffn1_w_out, mix_norm, w_in, w_out, diff_q_norm, diff_k_norm, diff_lambda_q1, diff_lambda_k1, diff_lambda_q2, diff_lambda_k2, diff_out_norm, gdn_conv_w, gdn_a_log, gdn_dt_bias, gdn_out_norm, ret_out_norm, dsa_q_norm, dsa_k_norm, ffn2_norm, ffn2_w_in, ffn2_w_out):
    b, seq, d = x.shape
    depth = w_in.shape[0]
    n_keep = min(TOPK_MAX, seq // 4)
    t = seq + N_META
    tp = -(-t // ATT_TILE) * ATT_TILE
    rows = b * tp

    h = jnp.concatenate([jnp.broadcast_to(meta_tokens.astype(x.dtype)[None], (b, N_META, d)), x,
                         jnp.zeros((b, tp - t, d), x.dtype)], axis=1).reshape(rows, d)
    tabs64, tabs128 = _rope_tables(tp)
    ret_consts = _retention_consts(ATT_TILE)
    row = lambda v: v.astype(f32)[None, :]
    twice = lambda v: jnp.concatenate([v, v]).astype(f32)[None, :]

    gains = lambda g: g.astype(f32)[:, None, :]
    ffn1 = (gains(ffn1_norm), ffn1_w_in.astype(bf16), ffn1_w_out.astype(bf16))
    ffn2 = (gains(ffn2_norm), ffn2_w_in.astype(bf16), ffn2_w_out.astype(bf16))
    w_in_p = _permute_w_in(w_in).astype(bf16)
    w_out_b = w_out.astype(bf16)
    mix_g = gains(mix_norm)

    for l in range(depth):
        lambda_init = 0.8 - 0.6 * math.exp(-0.3 * l)
        h = _ffn(h, *ffn1, l)
        proj = _proj(h, mix_g, w_in_p, l).reshape(b, tp, NP)

        qz, ka, va = _prep_a(proj, tabs64, twice(diff_q_norm[l]), twice(diff_k_norm[l]))
        y_a = _attn_a(qz, ka, va, row(diff_lambda_q1[l]), row(diff_lambda_k1[l]),
                      row(diff_lambda_q2[l]), row(diff_lambda_k2[l]), row(diff_out_norm[l]),
                      lambda_init)
        y_b = _gdn(proj, gdn_conv_w[l].astype(f32), _lane_row(gdn_a_log[l], SM_ALPHA),
                   _lane_row(gdn_dt_bias[l], SM_ALPHA), row(gdn_out_norm[l]))
        y_c = _retention(proj, tabs64, ret_consts, row(ret_out_norm[l]))
        qd, kd, vd, iq, ik = _prep_d(proj, tabs64, tabs128, row(dsa_q_norm[l]), row(dsa_k_norm[l]))
        y_d = _dsa(iq, proj, ik, qd, kd, vd, n_keep)

        ys = [y.reshape(rows, GROUP) for y in (y_a, y_b, y_c, y_d)]
        h = _outproj(h, ys, w_out_b, l)
        h = _ffn(h, *ffn2, l)
    return h.reshape(b, tp, d)[:, N_META:t]
```

```python
import functools
import math

import jax
import jax.numpy as jnp
import numpy as np
from jax import lax
from jax.experimental import pallas as pl
from jax.experimental.pallas import tpu as pltpu

f32 = jnp.float32
bf16 = jnp.bfloat16
HIGHEST = lax.Precision.HIGHEST

N_META = 16
HEAD_DIM = 128
HEADS = 4
GROUP = HEADS * HEAD_DIM
EPS = 1e-6
ROPE_THETA = 10000.0
NEG = -1e30
TOPK_MAX = 256
CONV_K = 4
IDX_HEADS = 16
IDX_DIM = 64
CHUNK = 64
LANES = 128
ATT_TILE = 384
VMEM_LIMIT = 56 * 1024 * 1024

NP = 8192
COL_A = 0
COL_B = 1536
COL_C = 3584
COL_D = 5120
COL_IK = 7680
COL_SM = 7808
SM_IW, SM_BETA, SM_ALPHA = 0, 16, 20

_NT = (((1,), (1,)), ((), ()))


def _cparams(sem):
    return pltpu.CompilerParams(dimension_semantics=sem, vmem_limit_bytes=VMEM_LIMIT)


def _row_tile(rows, target):
    best = 8
    for t in range(8, target + 1, 8):
        if rows % t == 0:
            best = t
    return best


def _silu(x):
    return x * jax.nn.sigmoid(x)


def _rms(x, g):
    return x * lax.rsqrt(jnp.mean(x * x, axis=-1, keepdims=True) + EPS) * g


def _ffn_kernel(h_ref, g_ref, wg_ref, wu_ref, wo_ref, o_ref, xn_ref):
    @pl.when(pl.program_id(1) == 0)
    def _():
        x = h_ref[...]
        xn_ref[...] = _rms(x, g_ref[...]).astype(bf16)
        o_ref[...] = x

    xn = xn_ref[...]
    gate = jnp.dot(xn, wg_ref[...], preferred_element_type=f32)
    up = jnp.dot(xn, wu_ref[...], preferred_element_type=f32)
    act = (_silu(gate) * up).astype(bf16)
    o_ref[...] += 0.5 * jnp.dot(act, wo_ref[...], preferred_element_type=f32)


def _ffn(h, g, w_in, w_out, l):
    rows, d = h.shape
    ff = w_out.shape[1]
    tm = _row_tile(rows, 768)
    tf = 512
    nf = ff // tf
    return pl.pallas_call(
        _ffn_kernel,
        grid=(rows // tm, nf),
        in_specs=[
            pl.BlockSpec((tm, d), lambda i, j: (i, 0)),
            pl.BlockSpec((None, 1, d), lambda i, j: (l, 0, 0)),
            pl.BlockSpec((None, d, tf), lambda i, j: (l, 0, j)),
            pl.BlockSpec((None, d, tf), lambda i, j: (l, 0, j + nf)),
            pl.BlockSpec((None, tf, d), lambda i, j: (l, j, 0)),
        ],
        out_specs=pl.BlockSpec((tm, d), lambda i, j: (i, 0)),
        out_shape=jax.ShapeDtypeStruct((rows, d), f32),
        scratch_shapes=[pltpu.VMEM((tm, d), bf16)],
        compiler_params=_cparams(("parallel", "arbitrary")),
        name="ffn",
    )(h, g, w_in, w_in, w_out)


def _proj_kernel(h_ref, g_ref, w_ref, o_ref, xn_ref):
    @pl.when(pl.program_id(1) == 0)
    def _():
        xn_ref[...] = _rms(h_ref[...], g_ref[...]).astype(bf16)

    o_ref[...] = jnp.dot(xn_ref[...], w_ref[...], preferred_element_type=f32)


def _proj(h, g, w, l):
    rows, d = h.shape
    n = w.shape[2]
    tm = _row_tile(rows, 768)
    tn = 1024
    return pl.pallas_call(
        _proj_kernel,
        grid=(rows // tm, n // tn),
        in_specs=[
            pl.BlockSpec((tm, d), lambda i, j: (i, 0)),
            pl.BlockSpec((None, 1, d), lambda i, j: (l, 0, 0)),
            pl.BlockSpec((None, d, tn), lambda i, j: (l, 0, j)),
        ],
        out_specs=pl.BlockSpec((tm, tn), lambda i, j: (i, j)),
        out_shape=jax.ShapeDtypeStruct((rows, n), f32),
        scratch_shapes=[pltpu.VMEM((tm, d), bf16)],
        compiler_params=_cparams(("parallel", "arbitrary")),
        name="proj",
    )(h, g, w)


def _outproj_kernel(h_ref, ya_ref, yb_ref, yc_ref, yd_ref, w_ref, o_ref):
    acc = h_ref[...]
    for n, y_ref in enumerate((ya_ref, yb_ref, yc_ref, yd_ref)):
        acc = acc + jnp.dot(y_ref[...], w_ref[n * GROUP:(n + 1) * GROUP, :],
                            preferred_element_type=f32)
    o_ref[...] = acc


def _outproj(h, ys, w, l):
    rows, d = h.shape
    tm = _row_tile(rows, 768)
    y_spec = pl.BlockSpec((tm, GROUP), lambda i: (i, 0))
    return pl.pallas_call(
        _outproj_kernel,
        grid=(rows // tm,),
        in_specs=[pl.BlockSpec((tm, d), lambda i: (i, 0)), y_spec, y_spec, y_spec, y_spec,
                  pl.BlockSpec((None, w.shape[1], d), lambda i: (l, 0, 0))],
        out_specs=pl.BlockSpec((tm, d), lambda i: (i, 0)),
        out_shape=jax.ShapeDtypeStruct((rows, d), f32),
        compiler_params=_cparams(("parallel",)),
        name="outproj",
    )(h, *ys, w)


def _rope64(x, c, sa, sb):
    return x * c + pltpu.roll(x, LANES - 32, 1) * sa + pltpu.roll(x, 32, 1) * sb


def _rope128(x, c, s):
    return x * c + pltpu.roll(x, 64, 1) * s


def _group_mean_matrix():
    r = lax.broadcasted_iota(jnp.int32, (LANES, LANES), 0) >> 6
    c = lax.broadcasted_iota(jnp.int32, (LANES, LANES), 1) >> 6
    return jnp.where(r == c, 1.0 / 64.0, 0.0).astype(f32)


def _prep_a_kernel(q_ref, k_ref, v_ref, c_ref, sa_ref, sb_ref, gq_ref, gk_ref,
                   qz_ref, ko_ref, vo_ref):
    tm = q_ref.shape[0]
    lane = lax.broadcasted_iota(jnp.int32, (tm, LANES), 1)
    gmat = _group_mean_matrix()
    c, sa, sb = c_ref[...], sa_ref[...], sb_ref[...]

    def norm_rope(x, g):
        ms = jnp.dot(x * x, gmat, precision=HIGHEST, preferred_element_type=f32)
        return _rope64(x * lax.rsqrt(ms + EPS) * g, c, sa, sb)

    for h in range(HEADS):
        sl = slice(h * HEAD_DIM, (h + 1) * HEAD_DIM)
        q = norm_rope(q_ref[:, sl], gq_ref[...]) * (64 ** -0.5)
        qz_ref[h, :, 0:tm] = jnp.where(lane < 64, q, 0.0).T.astype(bf16)
        qz_ref[h, :, tm:2 * tm] = jnp.where(lane >= 64, q, 0.0).T.astype(bf16)
        ko_ref[h] = norm_rope(k_ref[:, sl], gk_ref[...]).astype(bf16)
        vo_ref[h] = v_ref[:, sl].T.astype(bf16)


def _prep_a(proj, tabs64, gq, gk):
    b, tp, _ = proj.shape
    tm = ATT_TILE
    nt = tp // tm
    cb = COL_A // GROUP
    tab_spec = pl.BlockSpec((tm, LANES), lambda bi, i: (i, 0))
    g_spec = pl.BlockSpec((1, LANES), lambda bi, i: (0, 0))
    return pl.pallas_call(
        _prep_a_kernel,
        grid=(b, nt),
        in_specs=[
            pl.BlockSpec((None, tm, GROUP), lambda bi, i: (bi, i, cb)),
            pl.BlockSpec((None, tm, GROUP), lambda bi, i: (bi, i, cb + 1)),
            pl.BlockSpec((None, tm, GROUP), lambda bi, i: (bi, i, cb + 2)),
            tab_spec, tab_spec, tab_spec, g_spec, g_spec,
        ],
        out_specs=[
            pl.BlockSpec((None, HEADS, None, HEAD_DIM, 2 * tm), lambda bi, i: (bi, 0, i, 0, 0)),
            pl.BlockSpec((None, HEADS, tm, HEAD_DIM), lambda bi, i: (bi, 0, i, 0)),
            pl.BlockSpec((None, HEADS, None, HEAD_DIM, tm), lambda bi, i: (bi, 0, i, 0, 0)),
        ],
        out_shape=[
            jax.ShapeDtypeStruct((b, HEADS, nt, HEAD_DIM, 2 * tm), bf16),
            jax.ShapeDtypeStruct((b, HEADS, tp, HEAD_DIM), bf16),
            jax.ShapeDtypeStruct((b, HEADS, nt, HEAD_DIM, tm), bf16),
        ],
        compiler_params=_cparams(("parallel", "parallel")),
        name="prep_a",
    )(proj, proj, proj, *tabs64, gq, gk)


def _softmax_step(ss, vts, carries):
    n = range(len(ss))
    m_new = [jnp.maximum(carries[g][0], jnp.max(ss[g], axis=0, keepdims=True)) for g in n]
    alpha = [jnp.exp(carries[g][0] - m_new[g]) for g in n]
    p = [jnp.exp(ss[g] - m_new[g]) for g in n]
    l = [alpha[g] * carries[g][1] + jnp.sum(p[g], axis=0, keepdims=True) for g in n]
    acc = [alpha[g] * carries[g][2]
           + jnp.dot(vts[g], p[g].astype(bf16), preferred_element_type=f32) for g in n]
    return tuple((m_new[g], l[g], acc[g]) for g in n)


def _softmax_init(groups, width):
    return tuple((jnp.full((1, width), NEG, f32), jnp.zeros((1, width), f32),
                  jnp.zeros((HEAD_DIM, width), f32)) for _ in range(groups))


def _attn_a_kernel(qz_ref, k_ref, vt_ref, lq1_ref, lk1_ref, lq2_ref, lk2_ref, gn_ref, o_ref,
                   *, tq, lambda_init):
    i = pl.program_id(2)
    qz = qz_ref[...]
    row = lax.broadcasted_iota(jnp.int32, (tq, 2 * tq), 0)
    col = lax.broadcasted_iota(jnp.int32, (tq, 2 * tq), 1)
    causal = row <= jnp.where(col >= tq, col - tq, col)

    gw = 2 * LANES
    ng = 2 * tq // gw

    def scores(j):
        start = pl.multiple_of(j * tq, tq)
        kb = k_ref[pl.ds(start, tq), :]
        return tuple(jnp.dot(kb, qz[:, g * gw:(g + 1) * gw], preferred_element_type=f32)
                     for g in range(ng))

    def step(j, sm):
        return _softmax_step(scores(j), [vt_ref[j]] * ng, sm)

    sm = lax.fori_loop(0, i, step, _softmax_init(ng, gw))
    ss = scores(i)
    ss = [jnp.where(causal[:, g * gw:(g + 1) * gw], ss[g], NEG) for g in range(ng)]
    sm = _softmax_step(ss, [vt_ref[i]] * ng, sm)
    o = jnp.concatenate([acc * (1.0 / l) for _, l, acc in sm], axis=1)
    lam = (jnp.exp(jnp.sum(lq1_ref[...] * lk1_ref[...], axis=-1, keepdims=True))
           - jnp.exp(jnp.sum(lq2_ref[...] * lk2_ref[...], axis=-1, keepdims=True))
           + lambda_init)
    od = o[:, :tq] - lam * o[:, tq:]
    odn = od * lax.rsqrt(jnp.mean(od * od, axis=0, keepdims=True) + EPS)
    o_ref[...] = (odn.T * gn_ref[...] * (1.0 - lambda_init)).astype(bf16)


def _attn_a(qz, k, vt, lq1, lk1, lq2, lk2, gn, lambda_init):
    b, _, nt, _, _ = qz.shape
    tq = ATT_TILE
    tp = nt * tq
    l_spec = pl.BlockSpec((1, 64), lambda bi, h, i: (0, 0))
    return pl.pallas_call(
        functools.partial(_attn_a_kernel, tq=tq, lambda_init=lambda_init),
        grid=(b, HEADS, nt),
        in_specs=[
            pl.BlockSpec((None, None, None, HEAD_DIM, 2 * tq), lambda bi, h, i: (bi, h, i, 0, 0)),
            pl.BlockSpec((None, None, tp, HEAD_DIM), lambda bi, h, i: (bi, h, 0, 0)),
            pl.BlockSpec((None, None, nt, HEAD_DIM, tq), lambda bi, h, i: (bi, h, 0, 0, 0)),
            l_spec, l_spec, l_spec, l_spec,
            pl.BlockSpec((1, HEAD_DIM), lambda bi, h, i: (0, 0)),
        ],
        out_specs=pl.BlockSpec((None, tq, HEAD_DIM), lambda bi, h, i: (bi, i, h)),
        out_shape=jax.ShapeDtypeStruct((b, tp, GROUP), bf16),
        compiler_params=_cparams(("parallel", "parallel", "arbitrary")),
        name="attn_a",
    )(qz, k, vt, lq1, lk1, lq2, lk2, gn)


def _gdn_kernel(qkv_ref, z_ref, sm_ref, cw_ref, alog_ref, dtb_ref, gn_ref, o_ref,
                xx_ref, s_ref, *, tb):
    w3 = 3 * GROUP

    @pl.when(pl.program_id(1) == 0)
    def _():
        xx_ref[0:8, :] = jnp.zeros((8, w3), f32)
        s_ref[...] = jnp.zeros_like(s_ref)

    x = qkv_ref[...]
    xx_ref[8:tb + 8, :] = x
    cw = cw_ref[...]
    y = x * cw[CONV_K - 1:CONV_K, :]
    for s in range(1, CONV_K):
        y = y + xx_ref[pl.ds(8 - s, tb), :] * cw[CONV_K - 1 - s:CONV_K - s, :]
    xx_ref[0:8, :] = x[tb - 8:, :]
    y = _silu(y)

    sm = sm_ref[...]
    beta_all = jax.nn.sigmoid(sm)
    g_all = -jnp.exp(alog_ref[...]) * jax.nn.softplus(sm + dtb_ref[...])

    r = HEADS * CHUNK
    ri = lax.broadcasted_iota(jnp.int32, (r, r), 0)
    ci = lax.broadcasted_iota(jnp.int32, (r, r), 1)
    same = (ri >> 6) == (ci >> 6)
    tri = same & (ri >= ci)
    strict = same & (ri > ci)
    lmat = jnp.where(tri, 1.0, 0.0).astype(bf16)
    eye = jnp.where(ri == ci, 1.0, 0.0).astype(f32)
    own = ((lax.broadcasted_iota(jnp.int32, (r, GROUP), 0) >> 6)
           == (lax.broadcasted_iota(jnp.int32, (r, GROUP), 1) >> 7))
    gn = gn_ref[...]

    def bdot(a, b):
        return jnp.dot(a, b, preferred_element_type=f32)

    def split2(x):
        hi = x.astype(bf16)
        return hi, (x - hi.astype(f32)).astype(bf16)

    def split3(x):
        hi = x.astype(bf16)
        r1 = x - hi.astype(f32)
        mid = r1.astype(bf16)
        return hi, mid, (r1 - mid.astype(f32)).astype(bf16)

    def stack(fn):
        return jnp.concatenate([fn(h) for h in range(HEADS)], axis=0)

    def diag_blocks(x):
        return stack(lambda h: x[h * CHUNK:(h + 1) * CHUNK, h * HEAD_DIM:(h + 1) * HEAD_DIM])

    nc = tb // CHUNK
    chunks = range(nc)
    rows = [slice(c * CHUNK, (c + 1) * CHUNK) for c in chunks]

    def heads_of(c, base):
        return stack(lambda h: y[rows[c], base + h * HEAD_DIM:base + (h + 1) * HEAD_DIM])

    def gate_of(c, vals, lane0):
        return stack(lambda h: jnp.broadcast_to(vals[rows[c], lane0 + h:lane0 + h + 1],
                                                (CHUNK, LANES)))

    qs = [heads_of(c, 0) for c in chunks]
    ks = [heads_of(c, GROUP) for c in chunks]
    vs = [heads_of(c, 2 * GROUP) for c in chunks]
    qs = [q * lax.rsqrt(jnp.sum(q * q, axis=-1, keepdims=True) + EPS) * (HEAD_DIM ** -0.5)
          for q in qs]
    ks = [k * lax.rsqrt(jnp.sum(k * k, axis=-1, keepdims=True) + EPS) for k in ks]
    bs = [gate_of(c, beta_all, SM_BETA) for c in chunks]
    gs = [gate_of(c, g_all, SM_ALPHA) for c in chunks]
    gy = [bdot(lmat, jnp.concatenate(
        split3(jnp.where(strict, jnp.concatenate([g, g], axis=1), 0.0)) + split3(g), axis=1))
        for g in gs]
    diff = [(t[:, 2 * r:3 * r] + t[:, r:2 * r]) + t[:, :r] for t in gy]
    gcum = [(t[:, 3 * r + 2 * LANES:] + t[:, 3 * r + LANES:3 * r + 2 * LANES])
            + t[:, 3 * r:3 * r + LANES] for t in gy]
    decay = [jnp.where(tri, jnp.exp(t), 0.0) for t in diff]
    egc = [jnp.exp(t) for t in gcum]
    kb = [k * b_ for k, b_ in zip(ks, bs)]
    ksb = [k.astype(bf16) for k in ks]
    a_mat = [jnp.where(strict, lax.dot_general(kb[c].astype(bf16), ksb[c], _NT,
                                               preferred_element_type=f32) * decay[c], 0.0)
             for c in chunks]
    t_inv = [eye - a for a in a_mat]
    a_sp = [split2(a) for a in a_mat]
    r1 = [bdot(jnp.concatenate([a_h, a_l], axis=0), a_h) for a_h, a_l in a_sp]
    r2 = [bdot(a_h, a_l) for a_h, a_l in a_sp]
    pw = [(r1[c][r:] + r2[c]) + r1[c][:r] for c in chunks]
    n_sq = int(math.log2(CHUNK)) - 1
    for it in range(n_sq):
        p_sp = [split2(p) for p in pw]
        t_sp = [split2(t) for t in t_inv]
        if it + 1 < n_sq:
            r1 = [bdot(jnp.concatenate([t_sp[c][0], p_sp[c][0], t_sp[c][1], p_sp[c][1]], axis=0),
                       p_sp[c][0]) for c in chunks]
            r2 = [bdot(jnp.concatenate([t_sp[c][0], p_sp[c][0]], axis=0), p_sp[c][1])
                  for c in chunks]
            t_inv = [t_inv[c] + ((r1[c][2 * r:3 * r] + r2[c][:r]) + r1[c][:r]) for c in chunks]
            pw = [(r1[c][3 * r:] + r2[c][r:]) + r1[c][r:2 * r] for c in chunks]
        else:
            r1 = [bdot(jnp.concatenate([t_sp[c][0], t_sp[c][1]], axis=0), p_sp[c][0])
                  for c in chunks]
            r2 = [bdot(t_sp[c][0], p_sp[c][1]) for c in chunks]
            t_inv = [t_inv[c] + ((r1[c][r:] + r2[c]) + r1[c][:r]) for c in chunks]
    uw = [bdot(t_inv[c].astype(bf16),
               jnp.concatenate([vs[c] * bs[c], kb[c] * egc[c]], axis=1).astype(bf16))
          for c in chunks]
    attn = [(lax.dot_general(qs[c].astype(bf16), ksb[c], _NT, preferred_element_type=f32)
             * decay[c]).astype(bf16) for c in chunks]
    qg = [(qs[c] * egc[c]).astype(bf16) for c in chunks]
    last = lambda t, h: t[(h + 1) * CHUNK - 1:(h + 1) * CHUNK, :]
    kd = [(ks[c] * jnp.exp(stack(lambda h: jnp.broadcast_to(last(gcum[c], h), (CHUNK, LANES)))
                           - gcum[c])).T.astype(bf16) for c in chunks]
    elast = [jnp.concatenate([jnp.exp(last(gcum[c], h)) for h in range(HEADS)], axis=1)
             for c in chunks]

    state = s_ref[...]
    for c in chunks:
        u, w = uw[c][:, :HEAD_DIM], uw[c][:, HEAD_DIM:]
        sb = state.astype(bf16)
        v_new = u - diag_blocks(bdot(w.astype(bf16), sb))
        o = diag_blocks(bdot(qg[c], sb)) + bdot(attn[c], v_new.astype(bf16))
        v_bd = jnp.where(own, jnp.concatenate([v_new] * HEADS, axis=1), 0.0).astype(bf16)
        state = state * elast[c] + bdot(kd[c], v_bd)
        for h in range(HEADS):
            sl = slice(h * HEAD_DIM, (h + 1) * HEAD_DIM)
            zh = z_ref[rows[c], sl]
            o_ref[rows[c], sl] = (_rms(o[h * CHUNK:(h + 1) * CHUNK], gn) * _silu(zh)).astype(bf16)
    s_ref[...] = state


def _gdn(proj, conv_w, alog_row, dtb_row, gn):
    b, tp, _ = proj.shape
    tb = ATT_TILE
    w3 = 3 * GROUP
    return pl.pallas_call(
        functools.partial(_gdn_kernel, tb=tb),
        grid=(b, tp // tb),
        in_specs=[
            pl.BlockSpec((None, tb, w3), lambda bi, t: (bi, t, COL_B // w3)),
            pl.BlockSpec((None, tb, GROUP), lambda bi, t: (bi, t, (COL_B + w3) // GROUP)),
            pl.BlockSpec((None, tb, LANES), lambda bi, t: (bi, t, COL_SM // LANES)),
            pl.BlockSpec((CONV_K, w3), lambda bi, t: (0, 0)),
            pl.BlockSpec((1, LANES), lambda bi, t: (0, 0)),
            pl.BlockSpec((1, LANES), lambda bi, t: (0, 0)),
            pl.BlockSpec((1, HEAD_DIM), lambda bi, t: (0, 0)),
        ],
        out_specs=pl.BlockSpec((None, tb, GROUP), lambda bi, t: (bi, t, 0)),
        out_shape=jax.ShapeDtypeStruct((b, tp, GROUP), bf16),
        scratch_shapes=[pltpu.VMEM((tb + 8, w3), f32), pltpu.VMEM((HEAD_DIM, GROUP), f32)],
        compiler_params=_cparams(("parallel", "arbitrary")),
        name="gdn",
    )(proj, proj, proj, conv_w, alog_row, dtb_row, gn)


def _ret_kernel(qk_ref, v_ref, g_ref, c_ref, sa_ref, sb_ref, dm_ref, xi_ref, zt_ref, gch_ref,
                gn_ref, o_ref, r_ref, *, tc):
    @pl.when(pl.program_id(1) == 0)
    def _():
        r_ref[...] = jnp.zeros_like(r_ref)

    lane = lax.broadcasted_iota(jnp.int32, (tc, LANES), 1)
    c, sa, sb = c_ref[...], sa_ref[...], sb_ref[...]
    gn = gn_ref[...]
    for pair in range(HEADS // 2):
        qp = _rope64(qk_ref[:, pair * LANES:(pair + 1) * LANES], c, sa, sb)
        kp = _rope64(qk_ref[:, 2 * LANES + pair * LANES:2 * LANES + (pair + 1) * LANES],
                     c, sa, sb) * (64 ** -0.5)
        for half in range(2):
            h = 2 * pair + half
            sl = slice(h * HEAD_DIM, (h + 1) * HEAD_DIM)
            own = (lane >= 64) if half else (lane < 64)
            qm = jnp.where(own, qp, 0.0)
            km = jnp.where(own, kp, 0.0)
            vb = v_ref[:, sl].astype(bf16)
            inner = lax.dot_general(qm.astype(bf16), km.astype(bf16), _NT,
                                    preferred_element_type=f32) * dm_ref[h]
            r = r_ref[h]
            o = (jnp.dot(inner.astype(bf16), vb, preferred_element_type=f32)
                 + jnp.dot((qm * xi_ref[h]).astype(bf16), r.astype(bf16),
                           preferred_element_type=f32))
            kz = (km * zt_ref[h]).T.astype(bf16)
            r_ref[h] = r * gch_ref[h] + jnp.dot(kz, vb, preferred_element_type=f32)
            o_ref[:, sl] = (_silu(g_ref[:, sl]) * _rms(o, gn)).astype(bf16)


def _retention(proj, tabs64, consts, gn):
    b, tp, _ = proj.shape
    tc = ATT_TILE
    dmat, xi, zeta, gch = consts
    tab_spec = pl.BlockSpec((tc, LANES), lambda bi, t: (t, 0))
    cb = COL_C // GROUP
    full3 = lambda a: pl.BlockSpec(a.shape, lambda bi, t: (0, 0, 0))
    return pl.pallas_call(
        functools.partial(_ret_kernel, tc=tc),
        grid=(b, tp // tc),
        in_specs=[
            pl.BlockSpec((None, tc, GROUP), lambda bi, t: (bi, t, cb)),
            pl.BlockSpec((None, tc, GROUP), lambda bi, t: (bi, t, cb + 1)),
            pl.BlockSpec((None, tc, GROUP), lambda bi, t: (bi, t, cb + 2)),
            tab_spec, tab_spec, tab_spec,
            full3(dmat), full3(xi), full3(zeta), full3(gch),
            pl.BlockSpec((1, HEAD_DIM), lambda bi, t: (0, 0)),
        ],
        out_specs=pl.BlockSpec((None, tc, GROUP), lambda bi, t: (bi, t, 0)),
        out_shape=jax.ShapeDtypeStruct((b, tp, GROUP), bf16),
        scratch_shapes=[pltpu.VMEM((HEADS, HEAD_DIM, HEAD_DIM), f32)],
        compiler_params=_cparams(("parallel", "arbitrary")),
        name="retention",
    )(proj, proj, proj, *tabs64, dmat, xi, zeta, gch, gn)


def _retention_consts(tc):
    log_g = jnp.log(1.0 - 2.0 ** (-5.0 - jnp.arange(HEADS, dtype=f32)))
    i = jnp.arange(tc, dtype=f32)
    dist = i[:, None] - i[None, :]
    dmat = jnp.where(dist >= 0, jnp.exp(log_g[:, None, None] * jnp.maximum(dist, 0.0)), 0.0)
    ones = jnp.ones((1, 1, LANES), f32)
    xi = jnp.exp(log_g[:, None] * (i + 1.0))[..., None] * ones
    zeta = jnp.exp(log_g[:, None] * (tc - 1.0 - i))[..., None] * ones
    gch = jnp.exp(log_g * tc)[:, None, None] * ones
    return dmat, xi, zeta, gch


def _prep_d_kernel(q_ref, k_ref, v_ref, iq0_ref, iq1_ref, ik_ref, sm_ref, c64_ref, sa_ref, sb_ref,
                   c128_ref, s128_ref, gq_ref, gk_ref,
                   qo_ref, ko_ref, vo_ref, iqo_ref, iko_ref, iwo_ref):
    c64, sa, sb = c64_ref[...], sa_ref[...], sb_ref[...]
    c128, s128 = c128_ref[...], s128_ref[...]
    for h in range(HEADS):
        sl = slice(h * HEAD_DIM, (h + 1) * HEAD_DIM)
        q = _rope128(_rms(q_ref[:, sl], gq_ref[...]), c128, s128) * (HEAD_DIM ** -0.5)
        qo_ref[h] = q.T.astype(bf16)
        ko_ref[h] = _rope128(_rms(k_ref[:, sl], gk_ref[...]), c128, s128).astype(bf16)
        vo_ref[h] = v_ref[:, sl].T.astype(bf16)
    for p in range(IDX_HEADS // 2):
        src = iq0_ref if p < 4 else iq1_ref
        xt = _rope64(src[:, (p % 4) * LANES:(p % 4 + 1) * LANES], c64, sa, sb).T
        iqo_ref[2 * p] = xt[:IDX_DIM].astype(bf16)
        iqo_ref[2 * p + 1] = xt[IDX_DIM:].astype(bf16)
    iko_ref[...] = _rope64(ik_ref[...], c64, sa, sb)[:, :IDX_DIM].astype(bf16)
    iwo_ref[...] = sm_ref[...].T[SM_IW:SM_IW + IDX_HEADS]


def _prep_d(proj, tabs64, tabs128, gq, gk):
    b, tp, _ = proj.shape
    tm = ATT_TILE
    nt = tp // tm
    cb = COL_D // GROUP
    tab_spec = pl.BlockSpec((tm, LANES), lambda bi, i: (i, 0))
    g_spec = pl.BlockSpec((1, LANES), lambda bi, i: (0, 0))
    col = lambda k: pl.BlockSpec((None, tm, GROUP), lambda bi, i: (bi, i, cb + k))
    narrow = lambda c0: pl.BlockSpec((None, tm, LANES), lambda bi, i: (bi, i, c0 // LANES))
    tspec = pl.BlockSpec((None, HEADS, None, HEAD_DIM, tm), lambda bi, i: (bi, 0, i, 0, 0))
    tshape = jax.ShapeDtypeStruct((b, HEADS, nt, HEAD_DIM, tm), bf16)
    return pl.pallas_call(
        _prep_d_kernel,
        grid=(b, nt),
        in_specs=[col(0), col(1), col(2), col(3), col(4), narrow(COL_IK), narrow(COL_SM),
                  tab_spec, tab_spec, tab_spec, tab_spec, tab_spec, g_spec, g_spec],
        out_specs=[tspec,
                   pl.BlockSpec((None, HEADS, tm, HEAD_DIM), lambda bi, i: (bi, 0, i, 0)),
                   tspec,
                   pl.BlockSpec((None, None, IDX_HEADS, IDX_DIM, tm), lambda bi, i: (bi, i, 0, 0, 0)),
                   pl.BlockSpec((None, tm, IDX_DIM), lambda bi, i: (bi, i, 0)),
                   pl.BlockSpec((None, None, IDX_HEADS, tm), lambda bi, i: (bi, i, 0, 0))],
        out_shape=[tshape,
                   jax.ShapeDtypeStruct((b, HEADS, tp, HEAD_DIM), bf16),
                   tshape,
                   jax.ShapeDtypeStruct((b, nt, IDX_HEADS, IDX_DIM, tm), bf16),
                   jax.ShapeDtypeStruct((b, tp, IDX_DIM), bf16),
                   jax.ShapeDtypeStruct((b, nt, IDX_HEADS, tm), f32)],
        compiler_params=_cparams(("parallel", "parallel")),
        name="prep_d",
    )(proj, proj, proj, proj, proj, proj, proj, *tabs64, *tabs128, gq, gk)


def _key_to_f32(key):
    return lax.bitcast_convert_type(jnp.where(key < 0, key ^ 0x7FFFFFFF, key), f32)


def _dsa_kernel(iq_ref, iw_ref, ik_ref, q_ref, k_ref, vt_ref, o_ref, sc_ref, cs_ref,
                *, tq, n_keep):
    i = pl.program_id(1)
    nkb = i + 1
    row = lax.broadcasted_iota(jnp.int32, (tq, tq), 0)
    col = lax.broadcasted_iota(jnp.int32, (tq, tq), 1)
    causal = row <= col
    iw = iw_ref[...]
    kf = float(n_keep)
    int_min = -2 ** 31
    neg_inf_key = int_min + 0x7FFFFF

    def scores(j, masked):
        start = pl.multiple_of(j * tq, tq)
        ikb = ik_ref[pl.ds(start, tq), :]
        sc = jnp.zeros((tq, tq), f32)
        for hh in range(IDX_HEADS):
            s = jnp.dot(ikb, iq_ref[hh], preferred_element_type=f32)
            sc = sc + iw[hh:hh + 1, :] * jnp.maximum(s, 0.0)
        if masked:
            sc = jnp.where(causal, sc, -jnp.inf)
        sc_ref[j] = sc

    def _scores_body(j, carry):
        scores(j, False)
        return carry

    lax.fori_loop(0, i, _scores_body, 0)
    scores(i, True)

    def count(pred):
        def body(j, acc):
            return acc + jnp.sum(jnp.where(pred(sc_ref[j], j), 1.0, 0.0), axis=0, keepdims=True)
        return lax.fori_loop(0, nkb, body, jnp.zeros((1, tq), f32))

    def bisect(it, ans):
        cand = ans + lax.shift_left(jnp.int32(1), 31 - it)
        cand_f = _key_to_f32(cand)
        cnt = count(lambda blk, j: blk >= cand_f)
        return jnp.where(cnt >= kf, cand, ans)

    ans = lax.fori_loop(0, 32, bisect, jnp.full((1, tq), int_min, jnp.int32))
    thr_key = jnp.maximum(ans, neg_inf_key)
    thr = _key_to_f32(thr_key)

    n_ge = count(lambda blk, j: blk >= thr)
    n_gt = count(lambda blk, j: blk > thr)
    need = (n_ge > kf) & (thr_key > neg_inf_key)
    big = float(2 ** 30)
    cs_ref[...] = jnp.full((8, tq), big, f32)

    @pl.when(jnp.max(jnp.where(need, 1.0, 0.0)) > 0.0)
    def _():
        budget = kf - n_gt

        def tie_bisect(it, lo):
            cand = lo + lax.shift_left(jnp.int32(1), 14 - it).astype(f32)

            def pred(blk, j):
                keyg = (row + j * tq).astype(f32)
                return jnp.where(blk == thr, keyg, big) < cand
            cnt = count(pred)
            return jnp.where(cnt <= budget - 1.0, cand, lo)

        kstar = lax.fori_loop(0, 15, tie_bisect, jnp.zeros((1, tq), f32))
        cs_ref[...] = jnp.broadcast_to(jnp.where(need, kstar, big), (8, tq))

    kstar = cs_ref[0:1, :]

    def bias(j, masked):
        blk = sc_ref[j]
        keyg = (row + j * tq).astype(f32)
        sel = (blk > thr) | ((blk == thr) & (keyg <= kstar))
        if masked:
            sel = sel & causal
        sc_ref[j] = jnp.where(sel, 0.0, NEG)

    def _bias_body(j, carry):
        bias(j, False)
        return carry

    lax.fori_loop(0, i, _bias_body, 0)
    bias(i, True)

    qs = [q_ref[h] for h in range(HEADS)]

    def attend(j, sm):
        start = pl.multiple_of(j * tq, tq)
        b_ = sc_ref[j]
        ss = [jnp.dot(k_ref[h, pl.ds(start, tq), :], qs[h], preferred_element_type=f32) + b_
              for h in range(HEADS)]
        return _softmax_step(ss, [vt_ref[h, j] for h in range(HEADS)], sm)

    sm = lax.fori_loop(0, nkb, attend, _softmax_init(HEADS, tq))
    for h in range(HEADS):
        _, l, acc = sm[h]
        o_ref[:, h * HEAD_DIM:(h + 1) * HEAD_DIM] = (acc * (1.0 / l)).T.astype(bf16)


def _dsa(iqt, iwt, ik, qt, k, vt, n_keep):
    b, _, nq, _, tq = qt.shape
    assert n_keep <= tq
    tp = nq * tq
    return pl.pallas_call(
        functools.partial(_dsa_kernel, tq=tq, n_keep=n_keep),
        grid=(b, nq),
        in_specs=[
            pl.BlockSpec((None, None, IDX_HEADS, IDX_DIM, tq), lambda bi, i: (bi, i, 0, 0, 0)),
            pl.BlockSpec((None, None, IDX_HEADS, tq), lambda bi, i: (bi, i, 0, 0)),
            pl.BlockSpec((None, tp, IDX_DIM), lambda bi, i: (bi, 0, 0)),
            pl.BlockSpec((None, HEADS, None, HEAD_DIM, tq), lambda bi, i: (bi, 0, i, 0, 0)),
            pl.BlockSpec((None, HEADS, tp, HEAD_DIM), lambda bi, i: (bi, 0, 0, 0)),
            pl.BlockSpec((None, HEADS, nq, HEAD_DIM, tq), lambda bi, i: (bi, 0, 0, 0, 0)),
        ],
        out_specs=pl.BlockSpec((None, tq, GROUP), lambda bi, i: (bi, i, 0)),
        out_shape=jax.ShapeDtypeStruct((b, tp, GROUP), bf16),
        scratch_shapes=[pltpu.VMEM((nq, tq, tq), f32), pltpu.VMEM((8, tq), f32)],
        compiler_params=_cparams(("parallel", "arbitrary")),
        name="dsa",
    )(iqt, iwt, ik, qt, k, vt)


def _rope_tables(tp):
    pos = jnp.arange(tp, dtype=f32)

    def cs(d):
        inv = ROPE_THETA ** (-jnp.arange(0, d, 2, dtype=f32) / d)
        ang = pos[:, None] * inv[None, :]
        return jnp.cos(ang), jnp.sin(ang)

    c, s = cs(64)
    z = jnp.zeros_like(s)
    tabs64 = (jnp.tile(jnp.concatenate([c, c], axis=1), (1, 2)),
              jnp.tile(jnp.concatenate([-s, z], axis=1), (1, 2)),
              jnp.tile(jnp.concatenate([z, s], axis=1), (1, 2)))
    c, s = cs(128)
    tabs128 = (jnp.concatenate([c, c], axis=1), jnp.concatenate([-s, s], axis=1))
    return tabs64, tabs128


def _permute_w_in(w):
    z = lambda n: jnp.zeros(w.shape[:-1] + (n,), w.dtype)
    o_beta = 7 * GROUP
    o_c = o_beta + 2 * HEADS
    o_ik = o_c + 8 * GROUP
    o_iw = o_ik + IDX_DIM
    end = o_iw + IDX_HEADS
    return jnp.concatenate([
        w[..., :o_beta], w[..., o_c:o_ik], w[..., o_ik:o_iw], z(LANES - IDX_DIM),
        w[..., o_iw:end], w[..., o_beta:o_c], z(NP - COL_SM - IDX_HEADS - 2 * HEADS)], axis=-1)


def _lane_row(v, offset):
    return jnp.zeros((1, LANES), f32).at[0, offset:offset + v.shape[0]].set(v.astype(f32))


def kernel(x, meta_tokens, ffn1_norm, ffn1_w_in, ffn1_w_out, mix_norm, w_in, w_out, diff_q_norm, diff_k_norm, diff_lambda_q1, diff_lambda_k1, diff_lambda_q2, diff_lambda_k2, diff_out_norm, gdn_conv_w, gdn_a_log, gdn_dt_bias, gdn_out_norm, ret_out_norm, dsa_q_norm, dsa_k_norm, ffn2_norm, ffn2_w_in, ffn2_w_out):
    b, seq, d = x.shape
    depth = w_in.shape[0]
    n_keep = min(TOPK_MAX, seq // 4)
    t = seq + N_META
    tp = -(-t // ATT_TILE) * ATT_TILE
    rows = b * tp

    h = jnp.concatenate([jnp.broadcast_to(meta_tokens.astype(x.dtype)[None], (b, N_META, d)), x,
                         jnp.zeros((b, tp - t, d), x.dtype)], axis=1).reshape(rows, d)
    tabs64, tabs128 = _rope_tables(tp)
    ret_consts = _retention_consts(ATT_TILE)
    row = lambda v: v.astype(f32)[None, :]
    twice = lambda v: jnp.concatenate([v, v]).astype(f32)[None, :]

    gains = lambda g: g.astype(f32)[:, None, :]
    ffn1 = (gains(ffn1_norm), ffn1_w_in.astype(bf16), ffn1_w_out.astype(bf16))
    ffn2 = (gains(ffn2_norm), ffn2_w_in.astype(bf16), ffn2_w_out.astype(bf16))
    w_in_p = _permute_w_in(w_in).astype(bf16)
    w_out_b = w_out.astype(bf16)
    mix_g = gains(mix_norm)

    for l in range(depth):
        lambda_init = 0.8 - 0.6 * math.exp(-0.3 * l)
        h = _ffn(h, *ffn1, l)
        proj = _proj(h, mix_g, w_in_p, l).reshape(b, tp, NP)

        qz, ka, va = _prep_a(proj, tabs64, twice(diff_q_norm[l]), twice(diff_k_norm[l]))
        y_a = _attn_a(qz, ka, va, row(diff_lambda_q1[l]), row(diff_lambda_k1[l]),
                      row(diff_lambda_q2[l]), row(diff_lambda_k2[l]), row(diff_out_norm[l]),
                      lambda_init)
        y_b = _gdn(proj, gdn_conv_w[l].astype(f32), _lane_row(gdn_a_log[l], SM_ALPHA),
                   _lane_row(gdn_dt_bias[l], SM_ALPHA), row(gdn_out_norm[l]))
        y_c = _retention(proj, tabs64, ret_consts, row(ret_out_norm[l]))
        qd, kd, vd, iq, ik, iw = _prep_d(proj, tabs64, tabs128, row(dsa_q_norm[l]),
                                         row(dsa_k_norm[l]))
        y_d = _dsa(iq, iw, ik, qd, kd, vd, n_keep)

        ys = [y.reshape(rows, GROUP) for y in (y_a, y_b, y_c, y_d)]
        h = _outproj(h, ys, w_out_b, l)
        h = _ffn(h, *ffn2, l)
    return h.reshape(b, tp, d)[:, N_META:t]
```

```python
import functools
import math

import jax
import jax.numpy as jnp
import numpy as np
from jax import lax
from jax.experimental import pallas as pl
from jax.experimental.pallas import tpu as pltpu

f32 = jnp.float32
bf16 = jnp.bfloat16
HIGHEST = lax.Precision.HIGHEST

N_META = 16
HEAD_DIM = 128
HEADS = 4
GROUP = HEADS * HEAD_DIM
EPS = 1e-6
ROPE_THETA = 10000.0
NEG = -1e30
TOPK_MAX = 256
CONV_K = 4
IDX_HEADS = 16
IDX_DIM = 64
CHUNK = 64
LANES = 128
ATT_TILE = 384
VMEM_LIMIT = 56 * 1024 * 1024

NP = 8192
COL_A = 0
COL_B = 1536
COL_C = 3584
COL_D = 5120
COL_IK = 7680
COL_SM = 7808
SM_IW, SM_BETA, SM_ALPHA = 0, 16, 20

_NT = (((1,), (1,)), ((), ()))


def _cparams(sem):
    return pltpu.CompilerParams(dimension_semantics=sem, vmem_limit_bytes=VMEM_LIMIT)


def _row_tile(rows, target):
    best = 8
    for t in range(8, target + 1, 8):
        if rows % t == 0:
            best = t
    return best


def _silu(x):
    return x * jax.nn.sigmoid(x)


def _rms(x, g):
    return x * lax.rsqrt(jnp.mean(x * x, axis=-1, keepdims=True) + EPS) * g


def _ffn_kernel(h_ref, g_ref, wg_ref, wu_ref, wo_ref, o_ref, xn_ref):
    @pl.when(pl.program_id(1) == 0)
    def _():
        x = h_ref[...]
        xn_ref[...] = _rms(x, g_ref[...]).astype(bf16)
        o_ref[...] = x

    xn = xn_ref[...]
    gate = jnp.dot(xn, wg_ref[...], preferred_element_type=f32)
    up = jnp.dot(xn, wu_ref[...], preferred_element_type=f32)
    act = (_silu(gate) * up).astype(bf16)
    o_ref[...] += 0.5 * jnp.dot(act, wo_ref[...], preferred_element_type=f32)


def _ffn(h, g, w_in, w_out, l):
    rows, d = h.shape
    ff = w_out.shape[1]
    tm = _row_tile(rows, 768)
    tf = 512
    nf = ff // tf
    return pl.pallas_call(
        _ffn_kernel,
        grid=(rows // tm, nf),
        in_specs=[
            pl.BlockSpec((tm, d), lambda i, j: (i, 0)),
            pl.BlockSpec((None, 1, d), lambda i, j: (l, 0, 0)),
            pl.BlockSpec((None, d, tf), lambda i, j: (l, 0, j)),
            pl.BlockSpec((None, d, tf), lambda i, j: (l, 0, j + nf)),
            pl.BlockSpec((None, tf, d), lambda i, j: (l, j, 0)),
        ],
        out_specs=pl.BlockSpec((tm, d), lambda i, j: (i, 0)),
        out_shape=jax.ShapeDtypeStruct((rows, d), f32),
        scratch_shapes=[pltpu.VMEM((tm, d), bf16)],
        compiler_params=_cparams(("parallel", "arbitrary")),
        name="ffn",
    )(h, g, w_in, w_in, w_out)


def _proj_kernel(h_ref, g_ref, w_ref, o_ref, xn_ref):
    @pl.when(pl.program_id(1) == 0)
    def _():
        xn_ref[...] = _rms(h_ref[...], g_ref[...]).astype(bf16)

    o_ref[...] = jnp.dot(xn_ref[...], w_ref[...], preferred_element_type=f32)


def _proj(h, g, w, l):
    rows, d = h.shape
    n = w.shape[2]
    tm = _row_tile(rows, 768)
    tn = 1024
    return pl.pallas_call(
        _proj_kernel,
        grid=(rows // tm, n // tn),
        in_specs=[
            pl.BlockSpec((tm, d), lambda i, j: (i, 0)),
            pl.BlockSpec((None, 1, d), lambda i, j: (l, 0, 0)),
            pl.BlockSpec((None, d, tn), lambda i, j: (l, 0, j)),
        ],
        out_specs=pl.BlockSpec((tm, tn), lambda i, j: (i, j)),
        out_shape=jax.ShapeDtypeStruct((rows, n), f32),
        scratch_shapes=[pltpu.VMEM((tm, d), bf16)],
        compiler_params=_cparams(("parallel", "arbitrary")),
        name="proj",
    )(h, g, w)


def _outproj_kernel(h_ref, ya_ref, yb_ref, yc_ref, yd_ref, w_ref, o_ref):
    acc = h_ref[...]
    for n, y_ref in enumerate((ya_ref, yb_ref, yc_ref, yd_ref)):
        acc = acc + jnp.dot(y_ref[...], w_ref[n * GROUP:(n + 1) * GROUP, :],
                            preferred_element_type=f32)
    o_ref[...] = acc


def _outproj(h, ys, w, l):
    rows, d = h.shape
    tm = _row_tile(rows, 768)
    y_spec = pl.BlockSpec((tm, GROUP), lambda i: (i, 0))
    return pl.pallas_call(
        _outproj_kernel,
        grid=(rows // tm,),
        in_specs=[pl.BlockSpec((tm, d), lambda i: (i, 0)), y_spec, y_spec, y_spec, y_spec,
                  pl.BlockSpec((None, w.shape[1], d), lambda i: (l, 0, 0))],
        out_specs=pl.BlockSpec((tm, d), lambda i: (i, 0)),
        out_shape=jax.ShapeDtypeStruct((rows, d), f32),
        compiler_params=_cparams(("parallel",)),
        name="outproj",
    )(h, *ys, w)


def _rope64(x, c, sa, sb):
    return x * c + pltpu.roll(x, LANES - 32, 1) * sa + pltpu.roll(x, 32, 1) * sb


def _rope128(x, c, s):
    return x * c + pltpu.roll(x, 64, 1) * s


def _group_mean_matrix():
    r = lax.broadcasted_iota(jnp.int32, (LANES, LANES), 0) >> 6
    c = lax.broadcasted_iota(jnp.int32, (LANES, LANES), 1) >> 6
    return jnp.where(r == c, 1.0 / 64.0, 0.0).astype(f32)


def _prep_a_kernel(q_ref, k_ref, v_ref, c_ref, sa_ref, sb_ref, gq_ref, gk_ref,
                   qz_ref, ko_ref, vo_ref):
    tm = q_ref.shape[0]
    lane = lax.broadcasted_iota(jnp.int32, (tm, LANES), 1)
    gmat = _group_mean_matrix()
    c, sa, sb = c_ref[...], sa_ref[...], sb_ref[...]

    def norm_rope(x, g):
        ms = jnp.dot(x * x, gmat, precision=HIGHEST, preferred_element_type=f32)
        return _rope64(x * lax.rsqrt(ms + EPS) * g, c, sa, sb)

    for h in range(HEADS):
        sl = slice(h * HEAD_DIM, (h + 1) * HEAD_DIM)
        q = norm_rope(q_ref[:, sl], gq_ref[...]) * (64 ** -0.5)
        qz_ref[h, :, 0:tm] = jnp.where(lane < 64, q, 0.0).T.astype(bf16)
        qz_ref[h, :, tm:2 * tm] = jnp.where(lane >= 64, q, 0.0).T.astype(bf16)
        ko_ref[h] = norm_rope(k_ref[:, sl], gk_ref[...]).astype(bf16)
        vo_ref[h] = v_ref[:, sl].T.astype(bf16)


def _prep_a(proj, tabs64, gq, gk):
    b, tp, _ = proj.shape
    tm = ATT_TILE
    nt = tp // tm
    cb = COL_A // GROUP
    tab_spec = pl.BlockSpec((tm, LANES), lambda bi, i: (i, 0))
    g_spec = pl.BlockSpec((1, LANES), lambda bi, i: (0, 0))
    return pl.pallas_call(
        _prep_a_kernel,
        grid=(b, nt),
        in_specs=[
            pl.BlockSpec((None, tm, GROUP), lambda bi, i: (bi, i, cb)),
            pl.BlockSpec((None, tm, GROUP), lambda bi, i: (bi, i, cb + 1)),
            pl.BlockSpec((None, tm, GROUP), lambda bi, i: (bi, i, cb + 2)),
            tab_spec, tab_spec, tab_spec, g_spec, g_spec,
        ],
        out_specs=[
            pl.BlockSpec((None, HEADS, None, HEAD_DIM, 2 * tm), lambda bi, i: (bi, 0, i, 0, 0)),
            pl.BlockSpec((None, HEADS, tm, HEAD_DIM), lambda bi, i: (bi, 0, i, 0)),
            pl.BlockSpec((None, HEADS, None, HEAD_DIM, tm), lambda bi, i: (bi, 0, i, 0, 0)),
        ],
        out_shape=[
            jax.ShapeDtypeStruct((b, HEADS, nt, HEAD_DIM, 2 * tm), bf16),
            jax.ShapeDtypeStruct((b, HEADS, tp, HEAD_DIM), bf16),
            jax.ShapeDtypeStruct((b, HEADS, nt, HEAD_DIM, tm), bf16),
        ],
        compiler_params=_cparams(("parallel", "parallel")),
        name="prep_a",
    )(proj, proj, proj, *tabs64, gq, gk)


def _flash_blocks(nblk, qk, vt, mask_last, groups, width):
    gs = range(groups)

    def step(j, c, mask=None):
        ss = qk(j)
        if mask is not None:
            ss = mask(ss)
        out = []
        for g in gs:
            m, l, acc = c[g]
            m_new = jnp.maximum(m, jnp.max(ss[g], axis=0, keepdims=True))
            alpha = jnp.exp(m - m_new)
            p = jnp.exp(ss[g] - m_new)
            l = alpha * l + jnp.sum(p, axis=0, keepdims=True)
            acc = alpha * acc + jnp.dot(vt(g, j), p.astype(bf16), preferred_element_type=f32)
            out.append((m_new, l, acc))
        return tuple(out)

    init = tuple((jnp.full((1, width), NEG, f32), jnp.zeros((1, width), f32),
                  jnp.zeros((HEAD_DIM, width), f32)) for _ in gs)
    c = lax.fori_loop(0, nblk - 1, step, init)
    c = step(nblk - 1, c, mask_last)
    return tuple((l, acc) for _, l, acc in c)


def _attn_a_kernel(qz_ref, k_ref, vt_ref, lq1_ref, lk1_ref, lq2_ref, lk2_ref, gn_ref, o_ref,
                   *, tq, lambda_init):
    i = pl.program_id(2)
    qz = qz_ref[...]
    row = lax.broadcasted_iota(jnp.int32, (tq, 2 * tq), 0)
    col = lax.broadcasted_iota(jnp.int32, (tq, 2 * tq), 1)
    causal = row <= jnp.where(col >= tq, col - tq, col)

    gw = 2 * LANES
    ng = 2 * tq // gw

    def scores(j):
        start = pl.multiple_of(j * tq, tq)
        kb = k_ref[pl.ds(start, tq), :]
        return tuple(jnp.dot(kb, qz[:, g * gw:(g + 1) * gw], preferred_element_type=f32)
                     for g in range(ng))

    def mask_diag(ss):
        return tuple(jnp.where(causal[:, g * gw:(g + 1) * gw], ss[g], NEG) for g in range(ng))

    sm = _flash_blocks(i + 1, scores, lambda g, j: vt_ref[j], mask_diag, ng, gw)
    o = jnp.concatenate([acc * (1.0 / l) for l, acc in sm], axis=1)
    lam = (jnp.exp(jnp.sum(lq1_ref[...] * lk1_ref[...], axis=-1, keepdims=True))
           - jnp.exp(jnp.sum(lq2_ref[...] * lk2_ref[...], axis=-1, keepdims=True))
           + lambda_init)
    od = o[:, :tq] - lam * o[:, tq:]
    odn = od * lax.rsqrt(jnp.mean(od * od, axis=0, keepdims=True) + EPS)
    o_ref[...] = (odn.T * gn_ref[...] * (1.0 - lambda_init)).astype(bf16)


def _attn_a(qz, k, vt, lq1, lk1, lq2, lk2, gn, lambda_init):
    b, _, nt, _, _ = qz.shape
    tq = ATT_TILE
    tp = nt * tq
    l_spec = pl.BlockSpec((1, 64), lambda bi, h, i: (0, 0))
    return pl.pallas_call(
        functools.partial(_attn_a_kernel, tq=tq, lambda_init=lambda_init),
        grid=(b, HEADS, nt),
        in_specs=[
            pl.BlockSpec((None, None, None, HEAD_DIM, 2 * tq), lambda bi, h, i: (bi, h, i, 0, 0)),
            pl.BlockSpec((None, None, tp, HEAD_DIM), lambda bi, h, i: (bi, h, 0, 0)),
            pl.BlockSpec((None, None, nt, HEAD_DIM, tq), lambda bi, h, i: (bi, h, 0, 0, 0)),
            l_spec, l_spec, l_spec, l_spec,
            pl.BlockSpec((1, HEAD_DIM), lambda bi, h, i: (0, 0)),
        ],
        out_specs=pl.BlockSpec((None, tq, HEAD_DIM), lambda bi, h, i: (bi, i, h)),
        out_shape=jax.ShapeDtypeStruct((b, tp, GROUP), bf16),
        compiler_params=_cparams(("parallel", "parallel", "arbitrary")),
        name="attn_a",
    )(qz, k, vt, lq1, lk1, lq2, lk2, gn)


def _gdn_kernel(qkv_ref, z_ref, sm_ref, cw_ref, alog_ref, dtb_ref, gn_ref, o_ref,
                xx_ref, s_ref, *, tb):
    w3 = 3 * GROUP

    @pl.when(pl.program_id(1) == 0)
    def _():
        xx_ref[0:8, :] = jnp.zeros((8, w3), f32)
        s_ref[...] = jnp.zeros_like(s_ref)

    x = qkv_ref[...]
    xx_ref[8:tb + 8, :] = x
    cw = cw_ref[...]
    y = x * cw[CONV_K - 1:CONV_K, :]
    for s in range(1, CONV_K):
        y = y + xx_ref[pl.ds(8 - s, tb), :] * cw[CONV_K - 1 - s:CONV_K - s, :]
    xx_ref[0:8, :] = x[tb - 8:, :]
    y = _silu(y)

    sm = sm_ref[...]
    beta_all = jax.nn.sigmoid(sm)
    g_all = -jnp.exp(alog_ref[...]) * jax.nn.softplus(sm + dtb_ref[...])

    r = HEADS * CHUNK
    ri = lax.broadcasted_iota(jnp.int32, (r, r), 0)
    ci = lax.broadcasted_iota(jnp.int32, (r, r), 1)
    same = (ri >> 6) == (ci >> 6)
    tri = same & (ri >= ci)
    strict = same & (ri > ci)
    lmat = jnp.where(tri, 1.0, 0.0).astype(bf16)
    eye = jnp.where(ri == ci, 1.0, 0.0).astype(f32)
    own = ((lax.broadcasted_iota(jnp.int32, (r, GROUP), 0) >> 6)
           == (lax.broadcasted_iota(jnp.int32, (r, GROUP), 1) >> 7))
    gn = gn_ref[...]

    def bdot(a, b):
        return jnp.dot(a, b, preferred_element_type=f32)

    def split2(x):
        hi = x.astype(bf16)
        return hi, (x - hi.astype(f32)).astype(bf16)

    def split3(x):
        hi = x.astype(bf16)
        r1 = x - hi.astype(f32)
        mid = r1.astype(bf16)
        return hi, mid, (r1 - mid.astype(f32)).astype(bf16)

    def stack(fn):
        return jnp.concatenate([fn(h) for h in range(HEADS)], axis=0)

    def diag_blocks(x):
        return stack(lambda h: x[h * CHUNK:(h + 1) * CHUNK, h * HEAD_DIM:(h + 1) * HEAD_DIM])

    nc = tb // CHUNK
    chunks = range(nc)
    rows = [slice(c * CHUNK, (c + 1) * CHUNK) for c in chunks]

    def heads_of(c, base):
        return stack(lambda h: y[rows[c], base + h * HEAD_DIM:base + (h + 1) * HEAD_DIM])

    def gate_of(c, vals, lane0):
        return stack(lambda h: jnp.broadcast_to(vals[rows[c], lane0 + h:lane0 + h + 1],
                                                (CHUNK, LANES)))

    qs = [heads_of(c, 0) for c in chunks]
    ks = [heads_of(c, GROUP) for c in chunks]
    vs = [heads_of(c, 2 * GROUP) for c in chunks]
    qs = [q * lax.rsqrt(jnp.sum(q * q, axis=-1, keepdims=True) + EPS) * (HEAD_DIM ** -0.5)
          for q in qs]
    ks = [k * lax.rsqrt(jnp.sum(k * k, axis=-1, keepdims=True) + EPS) for k in ks]
    bs = [gate_of(c, beta_all, SM_BETA) for c in chunks]
    gs = [gate_of(c, g_all, SM_ALPHA) for c in chunks]
    gy = [bdot(lmat, jnp.concatenate(
        split3(jnp.where(strict, jnp.concatenate([g, g], axis=1), 0.0)) + split3(g), axis=1))
        for g in gs]
    diff = [(t[:, 2 * r:3 * r] + t[:, r:2 * r]) + t[:, :r] for t in gy]
    gcum = [(t[:, 3 * r + 2 * LANES:] + t[:, 3 * r + LANES:3 * r + 2 * LANES])
            + t[:, 3 * r:3 * r + LANES] for t in gy]
    decay = [jnp.where(tri, jnp.exp(t), 0.0) for t in diff]
    egc = [jnp.exp(t) for t in gcum]
    kb = [k * b_ for k, b_ in zip(ks, bs)]
    ksb = [k.astype(bf16) for k in ks]
    a_mat = [jnp.where(strict, lax.dot_general(kb[c].astype(bf16), ksb[c], _NT,
                                               preferred_element_type=f32) * decay[c], 0.0)
             for c in chunks]
    t_inv = [eye - a for a in a_mat]
    a_sp = [split2(a) for a in a_mat]
    r1 = [bdot(jnp.concatenate([a_h, a_l], axis=0), a_h) for a_h, a_l in a_sp]
    r2 = [bdot(a_h, a_l) for a_h, a_l in a_sp]
    pw = [(r1[c][r:] + r2[c]) + r1[c][:r] for c in chunks]
    n_sq = int(math.log2(CHUNK)) - 1
    for it in range(n_sq):
        p_sp = [split2(p) for p in pw]
        t_sp = [split2(t) for t in t_inv]
        if it + 1 < n_sq:
            r1 = [bdot(jnp.concatenate([t_sp[c][0], p_sp[c][0], t_sp[c][1], p_sp[c][1]], axis=0),
                       p_sp[c][0]) for c in chunks]
            r2 = [bdot(jnp.concatenate([t_sp[c][0], p_sp[c][0]], axis=0), p_sp[c][1])
                  for c in chunks]
            t_inv = [t_inv[c] + ((r1[c][2 * r:3 * r] + r2[c][:r]) + r1[c][:r]) for c in chunks]
            pw = [(r1[c][3 * r:] + r2[c][r:]) + r1[c][r:2 * r] for c in chunks]
        else:
            r1 = [bdot(jnp.concatenate([t_sp[c][0], t_sp[c][1]], axis=0), p_sp[c][0])
                  for c in chunks]
            r2 = [bdot(t_sp[c][0], p_sp[c][1]) for c in chunks]
            t_inv = [t_inv[c] + ((r1[c][r:] + r2[c]) + r1[c][:r]) for c in chunks]
    uw = [bdot(t_inv[c].astype(bf16),
               jnp.concatenate([vs[c] * bs[c], kb[c] * egc[c]], axis=1).astype(bf16))
          for c in chunks]
    attn = [(lax.dot_general(qs[c].astype(bf16), ksb[c], _NT, preferred_element_type=f32)
             * decay[c]).astype(bf16) for c in chunks]
    qg = [(qs[c] * egc[c]).astype(bf16) for c in chunks]
    last = lambda t, h: t[(h + 1) * CHUNK - 1:(h + 1) * CHUNK, :]
    kd = [(ks[c] * jnp.exp(stack(lambda h: jnp.broadcast_to(last(gcum[c], h), (CHUNK, LANES)))
                           - gcum[c])).T.astype(bf16) for c in chunks]
    elast = [jnp.concatenate([jnp.exp(last(gcum[c], h)) for h in range(HEADS)], axis=1)
             for c in chunks]

    state = s_ref[...]
    for c in chunks:
        u, w = uw[c][:, :HEAD_DIM], uw[c][:, HEAD_DIM:]
        sb = state.astype(bf16)
        v_new = u - diag_blocks(bdot(w.astype(bf16), sb))
        o = diag_blocks(bdot(qg[c], sb)) + bdot(attn[c], v_new.astype(bf16))
        v_bd = jnp.where(own, jnp.concatenate([v_new] * HEADS, axis=1), 0.0).astype(bf16)
        state = state * elast[c] + bdot(kd[c], v_bd)
        for h in range(HEADS):
            sl = slice(h * HEAD_DIM, (h + 1) * HEAD_DIM)
            zh = z_ref[rows[c], sl]
            o_ref[rows[c], sl] = (_rms(o[h * CHUNK:(h + 1) * CHUNK], gn) * _silu(zh)).astype(bf16)
    s_ref[...] = state


def _gdn(proj, conv_w, alog_row, dtb_row, gn):
    b, tp, _ = proj.shape
    tb = ATT_TILE
    w3 = 3 * GROUP
    return pl.pallas_call(
        functools.partial(_gdn_kernel, tb=tb),
        grid=(b, tp // tb),
        in_specs=[
            pl.BlockSpec((None, tb, w3), lambda bi, t: (bi, t, COL_B // w3)),
            pl.BlockSpec((None, tb, GROUP), lambda bi, t: (bi, t, (COL_B + w3) // GROUP)),
            pl.BlockSpec((None, tb, LANES), lambda bi, t: (bi, t, COL_SM // LANES)),
            pl.BlockSpec((CONV_K, w3), lambda bi, t: (0, 0)),
            pl.BlockSpec((1, LANES), lambda bi, t: (0, 0)),
            pl.BlockSpec((1, LANES), lambda bi, t: (0, 0)),
            pl.BlockSpec((1, HEAD_DIM), lambda bi, t: (0, 0)),
        ],
        out_specs=pl.BlockSpec((None, tb, GROUP), lambda bi, t: (bi, t, 0)),
        out_shape=jax.ShapeDtypeStruct((b, tp, GROUP), bf16),
        scratch_shapes=[pltpu.VMEM((tb + 8, w3), f32), pltpu.VMEM((HEAD_DIM, GROUP), f32)],
        compiler_params=_cparams(("parallel", "arbitrary")),
        name="gdn",
    )(proj, proj, proj, conv_w, alog_row, dtb_row, gn)


def _ret_kernel(qk_ref, v_ref, g_ref, c_ref, sa_ref, sb_ref, dm_ref, xi_ref, zt_ref, gch_ref,
                gn_ref, o_ref, r_ref, *, tc):
    @pl.when(pl.program_id(1) == 0)
    def _():
        r_ref[...] = jnp.zeros_like(r_ref)

    lane = lax.broadcasted_iota(jnp.int32, (tc, LANES), 1)
    c, sa, sb = c_ref[...], sa_ref[...], sb_ref[...]
    gn = gn_ref[...]
    for pair in range(HEADS // 2):
        qp = _rope64(qk_ref[:, pair * LANES:(pair + 1) * LANES], c, sa, sb)
        kp = _rope64(qk_ref[:, 2 * LANES + pair * LANES:2 * LANES + (pair + 1) * LANES],
                     c, sa, sb) * (64 ** -0.5)
        for half in range(2):
            h = 2 * pair + half
            sl = slice(h * HEAD_DIM, (h + 1) * HEAD_DIM)
            own = (lane >= 64) if half else (lane < 64)
            qm = jnp.where(own, qp, 0.0)
            km = jnp.where(own, kp, 0.0)
            vb = v_ref[:, sl].astype(bf16)
            inner = lax.dot_general(qm.astype(bf16), km.astype(bf16), _NT,
                                    preferred_element_type=f32) * dm_ref[h]
            r = r_ref[h]
            o = (jnp.dot(inner.astype(bf16), vb, preferred_element_type=f32)
                 + jnp.dot((qm * xi_ref[h]).astype(bf16), r.astype(bf16),
                           preferred_element_type=f32))
            kz = (km * zt_ref[h]).T.astype(bf16)
            r_ref[h] = r * gch_ref[h] + jnp.dot(kz, vb, preferred_element_type=f32)
            o_ref[:, sl] = (_silu(g_ref[:, sl]) * _rms(o, gn)).astype(bf16)


def _retention(proj, tabs64, consts, gn):
    b, tp, _ = proj.shape
    tc = ATT_TILE
    dmat, xi, zeta, gch = consts
    tab_spec = pl.BlockSpec((tc, LANES), lambda bi, t: (t, 0))
    cb = COL_C // GROUP
    full3 = lambda a: pl.BlockSpec(a.shape, lambda bi, t: (0, 0, 0))
    return pl.pallas_call(
        functools.partial(_ret_kernel, tc=tc),
        grid=(b, tp // tc),
        in_specs=[
            pl.BlockSpec((None, tc, GROUP), lambda bi, t: (bi, t, cb)),
            pl.BlockSpec((None, tc, GROUP), lambda bi, t: (bi, t, cb + 1)),
            pl.BlockSpec((None, tc, GROUP), lambda bi, t: (bi, t, cb + 2)),
            tab_spec, tab_spec, tab_spec,
            full3(dmat), full3(xi), full3(zeta), full3(gch),
            pl.BlockSpec((1, HEAD_DIM), lambda bi, t: (0, 0)),
        ],
        out_specs=pl.BlockSpec((None, tc, GROUP), lambda bi, t: (bi, t, 0)),
        out_shape=jax.ShapeDtypeStruct((b, tp, GROUP), bf16),
        scratch_shapes=[pltpu.VMEM((HEADS, HEAD_DIM, HEAD_DIM), f32)],
        compiler_params=_cparams(("parallel", "arbitrary")),
        name="retention",
    )(proj, proj, proj, *tabs64, dmat, xi, zeta, gch, gn)


def _retention_consts(tc):
    log_g = jnp.log(1.0 - 2.0 ** (-5.0 - jnp.arange(HEADS, dtype=f32)))
    i = jnp.arange(tc, dtype=f32)
    dist = i[:, None] - i[None, :]
    dmat = jnp.where(dist >= 0, jnp.exp(log_g[:, None, None] * jnp.maximum(dist, 0.0)), 0.0)
    ones = jnp.ones((1, 1, LANES), f32)
    xi = jnp.exp(log_g[:, None] * (i + 1.0))[..., None] * ones
    zeta = jnp.exp(log_g[:, None] * (tc - 1.0 - i))[..., None] * ones
    gch = jnp.exp(log_g * tc)[:, None, None] * ones
    return dmat, xi, zeta, gch


def _prep_d_kernel(q_ref, k_ref, v_ref, iq0_ref, iq1_ref, ik_ref, sm_ref, c64_ref, sa_ref, sb_ref,
                   c128_ref, s128_ref, gq_ref, gk_ref,
                   qo_ref, ko_ref, vo_ref, iqo_ref, iqr_ref, iko_ref, iwo_ref):
    c64, sa, sb = c64_ref[...], sa_ref[...], sb_ref[...]
    c128, s128 = c128_ref[...], s128_ref[...]
    for h in range(HEADS):
        sl = slice(h * HEAD_DIM, (h + 1) * HEAD_DIM)
        q = _rope128(_rms(q_ref[:, sl], gq_ref[...]), c128, s128) * (HEAD_DIM ** -0.5)
        qo_ref[h] = q.T.astype(bf16)
        ko_ref[h] = _rope128(_rms(k_ref[:, sl], gk_ref[...]), c128, s128).astype(bf16)
        vo_ref[h] = v_ref[:, sl].T.astype(bf16)
    mw = 2 * LANES
    for p in range(IDX_HEADS // 2):
        src = iq0_ref if p < 4 else iq1_ref
        xt = _rope64(src[:, (p % 4) * LANES:(p % 4 + 1) * LANES], c64, sa, sb).T.astype(bf16)
        iqo_ref[2 * p] = xt[:IDX_DIM, :mw]
        iqo_ref[2 * p + 1] = xt[IDX_DIM:, :mw]
        iqr_ref[p] = jnp.concatenate([xt[:IDX_DIM, mw:], xt[IDX_DIM:, mw:]], axis=1)
    iko_ref[...] = _rope64(ik_ref[...], c64, sa, sb)[:, :IDX_DIM].astype(bf16)
    iwo_ref[...] = sm_ref[...].T[SM_IW:SM_IW + IDX_HEADS]


def _prep_d(proj, tabs64, tabs128, gq, gk):
    b, tp, _ = proj.shape
    tm = ATT_TILE
    nt = tp // tm
    cb = COL_D // GROUP
    tab_spec = pl.BlockSpec((tm, LANES), lambda bi, i: (i, 0))
    g_spec = pl.BlockSpec((1, LANES), lambda bi, i: (0, 0))
    col = lambda k: pl.BlockSpec((None, tm, GROUP), lambda bi, i: (bi, i, cb + k))
    narrow = lambda c0: pl.BlockSpec((None, tm, LANES), lambda bi, i: (bi, i, c0 // LANES))
    tspec = pl.BlockSpec((None, HEADS, None, HEAD_DIM, tm), lambda bi, i: (bi, 0, i, 0, 0))
    tshape = jax.ShapeDtypeStruct((b, HEADS, nt, HEAD_DIM, tm), bf16)
    return pl.pallas_call(
        _prep_d_kernel,
        grid=(b, nt),
        in_specs=[col(0), col(1), col(2), col(3), col(4), narrow(COL_IK), narrow(COL_SM),
                  tab_spec, tab_spec, tab_spec, tab_spec, tab_spec, g_spec, g_spec],
        out_specs=[tspec,
                   pl.BlockSpec((None, HEADS, tm, HEAD_DIM), lambda bi, i: (bi, 0, i, 0)),
                   tspec,
                   pl.BlockSpec((None, None, IDX_HEADS, IDX_DIM, 2 * LANES),
                                lambda bi, i: (bi, i, 0, 0, 0)),
                   pl.BlockSpec((None, None, IDX_HEADS // 2, IDX_DIM, 2 * LANES),
                                lambda bi, i: (bi, i, 0, 0, 0)),
                   pl.BlockSpec((None, tm, IDX_DIM), lambda bi, i: (bi, i, 0)),
                   pl.BlockSpec((None, None, IDX_HEADS, tm), lambda bi, i: (bi, i, 0, 0))],
        out_shape=[tshape,
                   jax.ShapeDtypeStruct((b, HEADS, tp, HEAD_DIM), bf16),
                   tshape,
                   jax.ShapeDtypeStruct((b, nt, IDX_HEADS, IDX_DIM, 2 * LANES), bf16),
                   jax.ShapeDtypeStruct((b, nt, IDX_HEADS // 2, IDX_DIM, 2 * LANES), bf16),
                   jax.ShapeDtypeStruct((b, tp, IDX_DIM), bf16),
                   jax.ShapeDtypeStruct((b, nt, IDX_HEADS, tm), f32)],
        compiler_params=_cparams(("parallel", "parallel")),
        name="prep_d",
    )(proj, proj, proj, proj, proj, proj, proj, *tabs64, *tabs128, gq, gk)


def _key_to_f32(key):
    return lax.bitcast_convert_type(jnp.where(key < 0, key ^ 0x7FFFFFFF, key), f32)


def _dsa_kernel(iq_ref, iqr_ref, iw_ref, ik_ref, q_ref, k_ref, vt_ref, o_ref,
                sc_ref, sc16_ref, cs_ref, *, tq, n_keep):
    i = pl.program_id(1)
    nkb = i + 1
    row = lax.broadcasted_iota(jnp.int32, (tq, tq), 0)
    col = lax.broadcasted_iota(jnp.int32, (tq, tq), 1)
    causal = row <= col
    iw = iw_ref[...]
    kf = float(n_keep)
    int_min = -2 ** 31
    neg_inf_key = int_min + 0x7FFFFF

    mw = 2 * LANES

    def scores(j, masked):
        start = pl.multiple_of(j * tq, tq)
        ikb = ik_ref[pl.ds(start, tq), :]
        sc_a = jnp.zeros((tq, mw), f32)
        for hh in range(IDX_HEADS):
            s = jnp.dot(ikb, iq_ref[hh], preferred_element_type=f32)
            sc_a = sc_a + iw[hh:hh + 1, :mw] * jnp.maximum(s, 0.0)
        sc_b = jnp.zeros((tq, LANES), f32)
        for p in range(IDX_HEADS // 2):
            s = jnp.dot(ikb, iqr_ref[p], preferred_element_type=f32)
            sc_b = (sc_b + iw[2 * p:2 * p + 1, mw:] * jnp.maximum(s[:, :LANES], 0.0)
                    + iw[2 * p + 1:2 * p + 2, mw:] * jnp.maximum(s[:, LANES:], 0.0))
        sc = jnp.concatenate([sc_a, sc_b], axis=1)
        if masked:
            sc = jnp.where(causal, sc, -jnp.inf)
        sc_ref[j] = sc
        bits = lax.bitcast_convert_type(sc, jnp.int32) & -65536
        sc16_ref[j] = lax.bitcast_convert_type(bits, f32).astype(bf16)

    def _scores_body(j, carry):
        scores(j, False)
        return carry

    lax.fori_loop(0, i, _scores_body, 0)
    scores(i, True)

    def count(pred):
        def body(j, acc):
            return acc + jnp.sum(jnp.where(pred(sc_ref[j], j), 1.0, 0.0), axis=0, keepdims=True)
        return lax.fori_loop(0, nkb, body, jnp.zeros((1, tq), f32))

    pack = 16
    one16, zero16 = jnp.ones((), bf16), jnp.zeros((), bf16)

    def count16(c16):
        def body(j, acc):
            ind = jnp.where(sc16_ref[j] >= c16, one16, zero16)
            part = ind[0:pack]
            for t in range(1, tq // pack):
                part = part + ind[t * pack:(t + 1) * pack]
            return acc + jnp.sum(part.astype(f32), axis=0, keepdims=True)
        return lax.fori_loop(0, nkb, body, jnp.zeros((1, tq), f32))

    def bisect16(it, ans):
        cand = ans + lax.shift_left(jnp.int32(1), 31 - it)
        c16 = lax.bitcast_convert_type(
            jnp.where(cand < 0, cand ^ 0x7FFFFFFF, cand) & -65536, f32).astype(bf16)
        return jnp.where(count16(c16) >= kf, cand, ans)

    def bisect(it, ans):
        cand = ans + lax.shift_left(jnp.int32(1), 31 - it)
        cand_f = _key_to_f32(cand)
        cnt = count(lambda blk, j: blk >= cand_f)
        return jnp.where(cnt >= kf, cand, ans)

    ans = lax.fori_loop(0, 16, bisect16, jnp.full((1, tq), int_min, jnp.int32))
    ans = lax.fori_loop(16, 32, bisect, ans)
    thr_key = jnp.maximum(ans, neg_inf_key)
    thr = _key_to_f32(thr_key)

    n_ge = count(lambda blk, j: blk >= thr)
    need = (n_ge > kf) & (thr_key > neg_inf_key)
    big = float(2 ** 30)
    cs_ref[...] = jnp.full((8, tq), big, f32)

    @pl.when(jnp.max(jnp.where(need, 1.0, 0.0)) > 0.0)
    def _():
        budget = kf - count(lambda blk, j: blk > thr)

        def tie_bisect(it, lo):
            cand = lo + lax.shift_left(jnp.int32(1), 14 - it).astype(f32)

            def pred(blk, j):
                keyg = (row + j * tq).astype(f32)
                return jnp.where(blk == thr, keyg, big) < cand
            cnt = count(pred)
            return jnp.where(cnt <= budget - 1.0, cand, lo)

        kstar = lax.fori_loop(0, 15, tie_bisect, jnp.zeros((1, tq), f32))
        cs_ref[...] = jnp.broadcast_to(jnp.where(need, kstar, big), (8, tq))

    kstar = cs_ref[0:1, :]

    def bias(j, masked):
        blk = sc_ref[j]
        keyg = (row + j * tq).astype(f32)
        sel = (blk > thr) | ((blk == thr) & (keyg <= kstar))
        if masked:
            sel = sel & causal
        sc_ref[j] = jnp.where(sel, 0.0, NEG)

    def _bias_body(j, carry):
        bias(j, False)
        return carry

    lax.fori_loop(0, i, _bias_body, 0)
    bias(i, True)

    qs = [q_ref[h] for h in range(HEADS)]

    def scores_att(j):
        start = pl.multiple_of(j * tq, tq)
        b_ = sc_ref[j]
        return tuple(jnp.dot(k_ref[h, pl.ds(start, tq), :], qs[h], preferred_element_type=f32) + b_
                     for h in range(HEADS))

    sm = _flash_blocks(nkb, scores_att, lambda h, j: vt_ref[h, j], None, HEADS, tq)
    for h in range(HEADS):
        l, acc = sm[h]
        o_ref[:, h * HEAD_DIM:(h + 1) * HEAD_DIM] = (acc * (1.0 / l)).T.astype(bf16)


def _dsa(iqt, iqr, iwt, ik, qt, k, vt, n_keep):
    b, _, nq, _, tq = qt.shape
    assert n_keep <= tq and tq == 3 * LANES
    tp = nq * tq
    return pl.pallas_call(
        functools.partial(_dsa_kernel, tq=tq, n_keep=n_keep),
        grid=(b, nq),
        in_specs=[
            pl.BlockSpec((None, None, IDX_HEADS, IDX_DIM, 2 * LANES),
                         lambda bi, i: (bi, i, 0, 0, 0)),
            pl.BlockSpec((None, None, IDX_HEADS // 2, IDX_DIM, 2 * LANES),
                         lambda bi, i: (bi, i, 0, 0, 0)),
            pl.BlockSpec((None, None, IDX_HEADS, tq), lambda bi, i: (bi, i, 0, 0)),
            pl.BlockSpec((None, tp, IDX_DIM), lambda bi, i: (bi, 0, 0)),
            pl.BlockSpec((None, HEADS, None, HEAD_DIM, tq), lambda bi, i: (bi, 0, i, 0, 0)),
            pl.BlockSpec((None, HEADS, tp, HEAD_DIM), lambda bi, i: (bi, 0, 0, 0)),
            pl.BlockSpec((None, HEADS, nq, HEAD_DIM, tq), lambda bi, i: (bi, 0, 0, 0, 0)),
        ],
        out_specs=pl.BlockSpec((None, tq, GROUP), lambda bi, i: (bi, i, 0)),
        out_shape=jax.ShapeDtypeStruct((b, tp, GROUP), bf16),
        scratch_shapes=[pltpu.VMEM((nq, tq, tq), f32), pltpu.VMEM((nq, tq, tq), bf16),
                        pltpu.VMEM((8, tq), f32)],
        compiler_params=_cparams(("parallel", "arbitrary")),
        name="dsa",
    )(iqt, iqr, iwt, ik, qt, k, vt)


def _rope_tables(tp):
    pos = jnp.arange(tp, dtype=f32)

    def cs(d):
        inv = ROPE_THETA ** (-jnp.arange(0, d, 2, dtype=f32) / d)
        ang = pos[:, None] * inv[None, :]
        return jnp.cos(ang), jnp.sin(ang)

    c, s = cs(64)
    z = jnp.zeros_like(s)
    tabs64 = (jnp.tile(jnp.concatenate([c, c], axis=1), (1, 2)),
              jnp.tile(jnp.concatenate([-s, z], axis=1), (1, 2)),
              jnp.tile(jnp.concatenate([z, s], axis=1), (1, 2)))
    c, s = cs(128)
    tabs128 = (jnp.concatenate([c, c], axis=1), jnp.concatenate([-s, s], axis=1))
    return tabs64, tabs128


def _permute_w_in(w):
    z = lambda n: jnp.zeros(w.shape[:-1] + (n,), w.dtype)
    o_beta = 7 * GROUP
    o_c = o_beta + 2 * HEADS
    o_ik = o_c + 8 * GROUP
    o_iw = o_ik + IDX_DIM
    end = o_iw + IDX_HEADS
    return jnp.concatenate([
        w[..., :o_beta], w[..., o_c:o_ik], w[..., o_ik:o_iw], z(LANES - IDX_DIM),
        w[..., o_iw:end], w[..., o_beta:o_c], z(NP - COL_SM - IDX_HEADS - 2 * HEADS)], axis=-1)


def _lane_row(v, offset):
    return jnp.zeros((1, LANES), f32).at[0, offset:offset + v.shape[0]].set(v.astype(f32))


def kernel(x, meta_tokens, ffn1_norm, ffn1_w_in, ffn1_w_out, mix_norm, w_in, w_out, diff_q_norm, diff_k_norm, diff_lambda_q1, diff_lambda_k1, diff_lambda_q2, diff_lambda_k2, diff_out_norm, gdn_conv_w, gdn_a_log, gdn_dt_bias, gdn_out_norm, ret_out_norm, dsa_q_norm, dsa_k_norm, ffn2_norm, ffn2_w_in, ffn2_w_out):
    b, seq, d = x.shape
    depth = w_in.shape[0]
    n_keep = min(TOPK_MAX, seq // 4)
    t = seq + N_META
    tp = -(-t // ATT_TILE) * ATT_TILE
    rows = b * tp

    h = jnp.concatenate([jnp.broadcast_to(meta_tokens.astype(x.dtype)[None], (b, N_META, d)), x,
                         jnp.zeros((b, tp - t, d), x.dtype)], axis=1).reshape(rows, d)
    tabs64, tabs128 = _rope_tables(tp)
    ret_consts = _retention_consts(ATT_TILE)
    row = lambda v: v.astype(f32)[None, :]
    twice = lambda v: jnp.concatenate([v, v]).astype(f32)[None, :]

    gains = lambda g: g.astype(f32)[:, None, :]
    ffn1 = (gains(ffn1_norm), ffn1_w_in.astype(bf16), ffn1_w_out.astype(bf16))
    ffn2 = (gains(ffn2_norm), ffn2_w_in.astype(bf16), ffn2_w_out.astype(bf16))
    w_in_p = _permute_w_in(w_in).astype(bf16)
    w_out_b = w_out.astype(bf16)
    mix_g = gains(mix_norm)

    for l in range(depth):
        lambda_init = 0.8 - 0.6 * math.exp(-0.3 * l)
        h = _ffn(h, *ffn1, l)
        proj = _proj(h, mix_g, w_in_p, l).reshape(b, tp, NP)

        qz, ka, va = _prep_a(proj, tabs64, twice(diff_q_norm[l]), twice(diff_k_norm[l]))
        y_a = _attn_a(qz, ka, va, row(diff_lambda_q1[l]), row(diff_lambda_k1[l]),
                      row(diff_lambda_q2[l]), row(diff_lambda_k2[l]), row(diff_out_norm[l]),
                      lambda_init)
        y_b = _gdn(proj, gdn_conv_w[l].astype(f32), _lane_row(gdn_a_log[l], SM_ALPHA),
                   _lane_row(gdn_dt_bias[l], SM_ALPHA), row(gdn_out_norm[l]))
        y_c = _retention(proj, tabs64, ret_consts, row(ret_out_norm[l]))
        qd, kd, vd, iq, iqr, ik, iw = _prep_d(proj, tabs64, tabs128, row(dsa_q_norm[l]),
                                              row(dsa_k_norm[l]))
        y_d = _dsa(iq, iqr, iw, ik, qd, kd, vd, n_keep)

        ys = [y.reshape(rows, GROUP) for y in (y_a, y_b, y_c, y_d)]
        h = _outproj(h, ys, w_out_b, l)
        h = _ffn(h, *ffn2, l)
    return h.reshape(b, tp, d)[:, N_META:t]
```

```python
import functools
import math

import jax
import jax.numpy as jnp
import numpy as np
from jax import lax
from jax.experimental import pallas as pl
from jax.experimental.pallas import tpu as pltpu

f32 = jnp.float32
bf16 = jnp.bfloat16
HIGHEST = lax.Precision.HIGHEST

N_META = 16
HEAD_DIM = 128
HEADS = 4
GROUP = HEADS * HEAD_DIM
EPS = 1e-6
ROPE_THETA = 10000.0
NEG = -1e30
TOPK_MAX = 256
CONV_K = 4
IDX_HEADS = 16
IDX_DIM = 64
CHUNK = 64
LANES = 128
ATT_TILE = 384
VMEM_LIMIT = 56 * 1024 * 1024

NP = 8192
COL_A = 0
COL_B = 1536
COL_C = 3584
COL_D = 5120
COL_IK = 7680
COL_SM = 7808
SM_IW, SM_BETA, SM_ALPHA = 0, 16, 20

_NT = (((1,), (1,)), ((), ()))


def _cparams(sem):
    return pltpu.CompilerParams(dimension_semantics=sem, vmem_limit_bytes=VMEM_LIMIT)


def _row_tile(rows, target):
    best = 8
    for t in range(8, target + 1, 8):
        if rows % t == 0:
            best = t
    return best


def _silu(x):
    return x * jax.nn.sigmoid(x)


def _rms(x, g):
    return x * lax.rsqrt(jnp.mean(x * x, axis=-1, keepdims=True) + EPS) * g


def _ffn_kernel(h_ref, g_ref, wg_ref, wu_ref, wo_ref, o_ref, xn_ref):
    @pl.when(pl.program_id(1) == 0)
    def _():
        x = h_ref[...]
        xn_ref[...] = _rms(x, g_ref[...]).astype(bf16)
        o_ref[...] = x

    xn = xn_ref[...]
    gate = jnp.dot(xn, wg_ref[...], preferred_element_type=f32)
    up = jnp.dot(xn, wu_ref[...], preferred_element_type=f32)
    act = (_silu(gate) * up).astype(bf16)
    o_ref[...] += 0.5 * jnp.dot(act, wo_ref[...], preferred_element_type=f32)


def _ffn(h, g, w_in, w_out, l):
    rows, d = h.shape
    ff = w_out.shape[1]
    tm = _row_tile(rows, 768)
    tf = 512
    nf = ff // tf
    return pl.pallas_call(
        _ffn_kernel,
        grid=(rows // tm, nf),
        in_specs=[
            pl.BlockSpec((tm, d), lambda i, j: (i, 0)),
            pl.BlockSpec((None, 1, d), lambda i, j: (l, 0, 0)),
            pl.BlockSpec((None, d, tf), lambda i, j: (l, 0, j)),
            pl.BlockSpec((None, d, tf), lambda i, j: (l, 0, j + nf)),
            pl.BlockSpec((None, tf, d), lambda i, j: (l, j, 0)),
        ],
        out_specs=pl.BlockSpec((tm, d), lambda i, j: (i, 0)),
        out_shape=jax.ShapeDtypeStruct((rows, d), f32),
        scratch_shapes=[pltpu.VMEM((tm, d), bf16)],
        compiler_params=_cparams(("parallel", "arbitrary")),
        name="ffn",
    )(h, g, w_in, w_in, w_out)


def _proj_kernel(h_ref, g_ref, w_ref, o_ref, xn_ref):
    @pl.when(pl.program_id(1) == 0)
    def _():
        xn_ref[...] = _rms(h_ref[...], g_ref[...]).astype(bf16)

    o_ref[...] = jnp.dot(xn_ref[...], w_ref[...], preferred_element_type=f32)


def _proj(h, g, w, l):
    rows, d = h.shape
    n = w.shape[2]
    tm = _row_tile(rows, 768)
    tn = 2048
    return pl.pallas_call(
        _proj_kernel,
        grid=(rows // tm, n // tn),
        in_specs=[
            pl.BlockSpec((tm, d), lambda i, j: (i, 0)),
            pl.BlockSpec((None, 1, d), lambda i, j: (l, 0, 0)),
            pl.BlockSpec((None, d, tn), lambda i, j: (l, 0, j)),
        ],
        out_specs=pl.BlockSpec((tm, tn), lambda i, j: (i, j)),
        out_shape=jax.ShapeDtypeStruct((rows, n), f32),
        scratch_shapes=[pltpu.VMEM((tm, d), bf16)],
        compiler_params=_cparams(("parallel", "arbitrary")),
        name="proj",
    )(h, g, w)


def _outproj_kernel(h_ref, ya_ref, yb_ref, yc_ref, yd_ref, w_ref, o_ref):
    acc = h_ref[...]
    for n, y_ref in enumerate((ya_ref, yb_ref, yc_ref, yd_ref)):
        acc = acc + jnp.dot(y_ref[...], w_ref[n * GROUP:(n + 1) * GROUP, :],
                            preferred_element_type=f32)
    o_ref[...] = acc


def _outproj(h, ys, w, l):
    rows, d = h.shape
    tm = _row_tile(rows, 768)
    y_spec = pl.BlockSpec((tm, GROUP), lambda i: (i, 0))
    return pl.pallas_call(
        _outproj_kernel,
        grid=(rows // tm,),
        in_specs=[pl.BlockSpec((tm, d), lambda i: (i, 0)), y_spec, y_spec, y_spec, y_spec,
                  pl.BlockSpec((None, w.shape[1], d), lambda i: (l, 0, 0))],
        out_specs=pl.BlockSpec((tm, d), lambda i: (i, 0)),
        out_shape=jax.ShapeDtypeStruct((rows, d), f32),
        compiler_params=_cparams(("parallel",)),
        name="outproj",
    )(h, *ys, w)


def _rope64(x, c, sa, sb):
    return x * c + pltpu.roll(x, LANES - 32, 1) * sa + pltpu.roll(x, 32, 1) * sb


def _rope128(x, c, s):
    return x * c + pltpu.roll(x, 64, 1) * s


def _group_mean_matrix():
    r = lax.broadcasted_iota(jnp.int32, (LANES, LANES), 0) >> 6
    c = lax.broadcasted_iota(jnp.int32, (LANES, LANES), 1) >> 6
    return jnp.where(r == c, 1.0 / 64.0, 0.0).astype(f32)


def _prep_a_kernel(q_ref, k_ref, v_ref, c_ref, sa_ref, sb_ref, gq_ref, gk_ref,
                   qz_ref, ko_ref, vo_ref):
    tm = q_ref.shape[0]
    lane = lax.broadcasted_iota(jnp.int32, (tm, LANES), 1)
    gmat = _group_mean_matrix()
    c, sa, sb = c_ref[...], sa_ref[...], sb_ref[...]

    def norm_rope(x, g):
        ms = jnp.dot(x * x, gmat, precision=HIGHEST, preferred_element_type=f32)
        return _rope64(x * lax.rsqrt(ms + EPS) * g, c, sa, sb)

    for h in range(HEADS):
        sl = slice(h * HEAD_DIM, (h + 1) * HEAD_DIM)
        q = norm_rope(q_ref[:, sl], gq_ref[...]) * (64 ** -0.5)
        qz_ref[h, :, 0:tm] = jnp.where(lane < 64, q, 0.0).T.astype(bf16)
        qz_ref[h, :, tm:2 * tm] = jnp.where(lane >= 64, q, 0.0).T.astype(bf16)
        ko_ref[h] = norm_rope(k_ref[:, sl], gk_ref[...]).astype(bf16)
        vo_ref[h] = v_ref[:, sl].T.astype(bf16)


def _prep_a(proj, tabs64, gq, gk):
    b, tp, _ = proj.shape
    tm = ATT_TILE
    nt = tp // tm
    cb = COL_A // GROUP
    tab_spec = pl.BlockSpec((tm, LANES), lambda bi, i: (i, 0))
    g_spec = pl.BlockSpec((1, LANES), lambda bi, i: (0, 0))
    return pl.pallas_call(
        _prep_a_kernel,
        grid=(b, nt),
        in_specs=[
            pl.BlockSpec((None, tm, GROUP), lambda bi, i: (bi, i, cb)),
            pl.BlockSpec((None, tm, GROUP), lambda bi, i: (bi, i, cb + 1)),
            pl.BlockSpec((None, tm, GROUP), lambda bi, i: (bi, i, cb + 2)),
            tab_spec, tab_spec, tab_spec, g_spec, g_spec,
        ],
        out_specs=[
            pl.BlockSpec((None, HEADS, None, HEAD_DIM, 2 * tm), lambda bi, i: (bi, 0, i, 0, 0)),
            pl.BlockSpec((None, HEADS, tm, HEAD_DIM), lambda bi, i: (bi, 0, i, 0)),
            pl.BlockSpec((None, HEADS, None, HEAD_DIM, tm), lambda bi, i: (bi, 0, i, 0, 0)),
        ],
        out_shape=[
            jax.ShapeDtypeStruct((b, HEADS, nt, HEAD_DIM, 2 * tm), bf16),
            jax.ShapeDtypeStruct((b, HEADS, tp, HEAD_DIM), bf16),
            jax.ShapeDtypeStruct((b, HEADS, nt, HEAD_DIM, tm), bf16),
        ],
        compiler_params=_cparams(("parallel", "parallel")),
        name="prep_a",
    )(proj, proj, proj, *tabs64, gq, gk)


def _flash_blocks(nblk, qk, vt, mask_last, groups, width):
    gs = range(groups)

    def update(js, c, mask=None):
        tiles = [qk(j) for j in js]
        if mask is not None:
            tiles[-1] = mask(tiles[-1])
        out = []
        for g in gs:
            m, l, acc = c[g]
            m_new = m
            for t in tiles:
                m_new = jnp.maximum(m_new, jnp.max(t[g], axis=0, keepdims=True))
            alpha = jnp.exp(m - m_new)
            l = alpha * l
            acc = alpha * acc
            for j, t in zip(js, tiles):
                p = jnp.exp(t[g] - m_new)
                l = l + jnp.sum(p, axis=0, keepdims=True)
                acc = acc + jnp.dot(vt(g, j), p.astype(bf16), preferred_element_type=f32)
            out.append((m_new, l, acc))
        return tuple(out)

    init = tuple((jnp.full((1, width), NEG, f32), jnp.zeros((1, width), f32),
                  jnp.zeros((HEAD_DIM, width), f32)) for _ in gs)
    last = nblk - 1
    c = lax.fori_loop(0, last // 2, lambda t, c: update([2 * t, 2 * t + 1], c), init)
    c = lax.cond(last % 2 == 1,
                 lambda c: update([last - 1, last], c, mask_last),
                 lambda c: update([last], c, mask_last), c)
    return tuple((l, acc) for _, l, acc in c)


def _attn_a_kernel(qz_ref, k_ref, vt_ref, lq1_ref, lk1_ref, lq2_ref, lk2_ref, gn_ref, o_ref,
                   *, tq, lambda_init):
    i = pl.program_id(2)
    qz = qz_ref[...]
    row = lax.broadcasted_iota(jnp.int32, (tq, 2 * tq), 0)
    col = lax.broadcasted_iota(jnp.int32, (tq, 2 * tq), 1)
    causal = row <= jnp.where(col >= tq, col - tq, col)

    gw = 2 * LANES
    ng = 2 * tq // gw

    def scores(j):
        start = pl.multiple_of(j * tq, tq)
        kb = k_ref[pl.ds(start, tq), :]
        return tuple(jnp.dot(kb, qz[:, g * gw:(g + 1) * gw], preferred_element_type=f32)
                     for g in range(ng))

    def mask_diag(ss):
        return tuple(jnp.where(causal[:, g * gw:(g + 1) * gw], ss[g], NEG) for g in range(ng))

    sm = _flash_blocks(i + 1, scores, lambda g, j: vt_ref[j], mask_diag, ng, gw)
    o = jnp.concatenate([acc * (1.0 / l) for l, acc in sm], axis=1)
    lam = (jnp.exp(jnp.sum(lq1_ref[...] * lk1_ref[...], axis=-1, keepdims=True))
           - jnp.exp(jnp.sum(lq2_ref[...] * lk2_ref[...], axis=-1, keepdims=True))
           + lambda_init)
    od = o[:, :tq] - lam * o[:, tq:]
    odn = od * lax.rsqrt(jnp.mean(od * od, axis=0, keepdims=True) + EPS)
    o_ref[...] = (odn.T * gn_ref[...] * (1.0 - lambda_init)).astype(bf16)


def _attn_a(qz, k, vt, lq1, lk1, lq2, lk2, gn, lambda_init):
    b, _, nt, _, _ = qz.shape
    tq = ATT_TILE
    tp = nt * tq
    l_spec = pl.BlockSpec((1, 64), lambda bi, h, i: (0, 0))
    return pl.pallas_call(
        functools.partial(_attn_a_kernel, tq=tq, lambda_init=lambda_init),
        grid=(b, HEADS, nt),
        in_specs=[
            pl.BlockSpec((None, None, None, HEAD_DIM, 2 * tq), lambda bi, h, i: (bi, h, i, 0, 0)),
            pl.BlockSpec((None, None, tp, HEAD_DIM), lambda bi, h, i: (bi, h, 0, 0)),
            pl.BlockSpec((None, None, nt, HEAD_DIM, tq), lambda bi, h, i: (bi, h, 0, 0, 0)),
            l_spec, l_spec, l_spec, l_spec,
            pl.BlockSpec((1, HEAD_DIM), lambda bi, h, i: (0, 0)),
        ],
        out_specs=pl.BlockSpec((None, tq, HEAD_DIM), lambda bi, h, i: (bi, i, h)),
        out_shape=jax.ShapeDtypeStruct((b, tp, GROUP), bf16),
        compiler_params=_cparams(("parallel", "parallel", "arbitrary")),
        name="attn_a",
    )(qz, k, vt, lq1, lk1, lq2, lk2, gn)


def _gdn_kernel(qkv_ref, z_ref, sm_ref, cw_ref, alog_ref, dtb_ref, gn_ref, o_ref,
                xx_ref, s_ref, *, tb):
    w3 = 3 * GROUP

    @pl.when(pl.program_id(1) == 0)
    def _():
        xx_ref[0:8, :] = jnp.zeros((8, w3), f32)
        s_ref[...] = jnp.zeros_like(s_ref)

    x = qkv_ref[...]
    xx_ref[8:tb + 8, :] = x
    cw = cw_ref[...]
    y = x * cw[CONV_K - 1:CONV_K, :]
    for s in range(1, CONV_K):
        y = y + xx_ref[pl.ds(8 - s, tb), :] * cw[CONV_K - 1 - s:CONV_K - s, :]
    xx_ref[0:8, :] = x[tb - 8:, :]
    y = _silu(y)

    sm = sm_ref[...]
    beta_all = jax.nn.sigmoid(sm)
    g_all = -jnp.exp(alog_ref[...]) * jax.nn.softplus(sm + dtb_ref[...])

    r = HEADS * CHUNK
    ri = lax.broadcasted_iota(jnp.int32, (r, r), 0)
    ci = lax.broadcasted_iota(jnp.int32, (r, r), 1)
    same = (ri >> 6) == (ci >> 6)
    tri = same & (ri >= ci)
    strict = same & (ri > ci)
    lmat = jnp.where(tri, 1.0, 0.0).astype(bf16)
    eye = jnp.where(ri == ci, 1.0, 0.0).astype(f32)
    own = ((lax.broadcasted_iota(jnp.int32, (r, GROUP), 0) >> 6)
           == (lax.broadcasted_iota(jnp.int32, (r, GROUP), 1) >> 7))
    gn = gn_ref[...]

    def bdot(a, b):
        return jnp.dot(a, b, preferred_element_type=f32)

    def split2(x):
        hi = x.astype(bf16)
        return hi, (x - hi.astype(f32)).astype(bf16)

    def split3(x):
        hi = x.astype(bf16)
        r1 = x - hi.astype(f32)
        mid = r1.astype(bf16)
        return hi, mid, (r1 - mid.astype(f32)).astype(bf16)

    def stack(fn):
        return jnp.concatenate([fn(h) for h in range(HEADS)], axis=0)

    def diag_blocks(x):
        return stack(lambda h: x[h * CHUNK:(h + 1) * CHUNK, h * HEAD_DIM:(h + 1) * HEAD_DIM])

    nc = tb // CHUNK
    chunks = range(nc)
    rows = [slice(c * CHUNK, (c + 1) * CHUNK) for c in chunks]

    def heads_of(c, base):
        return stack(lambda h: y[rows[c], base + h * HEAD_DIM:base + (h + 1) * HEAD_DIM])

    def gate_of(c, vals, lane0):
        return stack(lambda h: jnp.broadcast_to(vals[rows[c], lane0 + h:lane0 + h + 1],
                                                (CHUNK, LANES)))

    qs = [heads_of(c, 0) for c in chunks]
    ks = [heads_of(c, GROUP) for c in chunks]
    vs = [heads_of(c, 2 * GROUP) for c in chunks]
    qs = [q * lax.rsqrt(jnp.sum(q * q, axis=-1, keepdims=True) + EPS) * (HEAD_DIM ** -0.5)
          for q in qs]
    ks = [k * lax.rsqrt(jnp.sum(k * k, axis=-1, keepdims=True) + EPS) for k in ks]
    bs = [gate_of(c, beta_all, SM_BETA) for c in chunks]
    gs = [gate_of(c, g_all, SM_ALPHA) for c in chunks]
    gy = [bdot(lmat, jnp.concatenate(
        split3(jnp.where(strict, jnp.concatenate([g, g], axis=1), 0.0)) + split3(g), axis=1))
        for g in gs]
    diff = [(t[:, 2 * r:3 * r] + t[:, r:2 * r]) + t[:, :r] for t in gy]
    gcum = [(t[:, 3 * r + 2 * LANES:] + t[:, 3 * r + LANES:3 * r + 2 * LANES])
            + t[:, 3 * r:3 * r + LANES] for t in gy]
    decay = [jnp.where(tri, jnp.exp(t), 0.0) for t in diff]
    egc = [jnp.exp(t) for t in gcum]
    kb = [k * b_ for k, b_ in zip(ks, bs)]
    ksb = [k.astype(bf16) for k in ks]
    a_mat = [jnp.where(strict, lax.dot_general(kb[c].astype(bf16), ksb[c], _NT,
                                               preferred_element_type=f32) * decay[c], 0.0)
             for c in chunks]
    t_inv = [eye - a for a in a_mat]
    a_sp = [split2(a) for a in a_mat]
    r1 = [bdot(jnp.concatenate([a_h, a_l], axis=0), a_h) for a_h, a_l in a_sp]
    r2 = [bdot(a_h, a_l) for a_h, a_l in a_sp]
    pw = [(r1[c][r:] + r2[c]) + r1[c][:r] for c in chunks]
    n_sq = int(math.log2(CHUNK)) - 1
    for it in range(n_sq):
        p_sp = [split2(p) for p in pw]
        t_sp = [split2(t) for t in t_inv]
        if it + 1 < n_sq:
            r1 = [bdot(jnp.concatenate([t_sp[c][0], p_sp[c][0], t_sp[c][1], p_sp[c][1]], axis=0),
                       p_sp[c][0]) for c in chunks]
            r2 = [bdot(jnp.concatenate([t_sp[c][0], p_sp[c][0]], axis=0), p_sp[c][1])
                  for c in chunks]
            t_inv = [t_inv[c] + ((r1[c][2 * r:3 * r] + r2[c][:r]) + r1[c][:r]) for c in chunks]
            pw = [(r1[c][3 * r:] + r2[c][r:]) + r1[c][r:2 * r] for c in chunks]
        else:
            r1 = [bdot(jnp.concatenate([t_sp[c][0], t_sp[c][1]], axis=0), p_sp[c][0])
                  for c in chunks]
            r2 = [bdot(t_sp[c][0], p_sp[c][1]) for c in chunks]
            t_inv = [t_inv[c] + ((r1[c][r:] + r2[c]) + r1[c][:r]) for c in chunks]
    uw = [bdot(t_inv[c].astype(bf16),
               jnp.concatenate([vs[c] * bs[c], kb[c] * egc[c]], axis=1).astype(bf16))
          for c in chunks]
    attn = [(lax.dot_general(qs[c].astype(bf16), ksb[c], _NT, preferred_element_type=f32)
             * decay[c]).astype(bf16) for c in chunks]
    qg = [(qs[c] * egc[c]).astype(bf16) for c in chunks]
    last = lambda t, h: t[(h + 1) * CHUNK - 1:(h + 1) * CHUNK, :]
    kd = [(ks[c] * jnp.exp(stack(lambda h: jnp.broadcast_to(last(gcum[c], h), (CHUNK, LANES)))
                           - gcum[c])).T.astype(bf16) for c in chunks]
    elast = [jnp.concatenate([jnp.exp(last(gcum[c], h)) for h in range(HEADS)], axis=1)
             for c in chunks]

    state = s_ref[...]
    for c in chunks:
        u, w = uw[c][:, :HEAD_DIM], uw[c][:, HEAD_DIM:]
        sb = state.astype(bf16)
        v_new = u - diag_blocks(bdot(w.astype(bf16), sb))
        o = diag_blocks(bdot(qg[c], sb)) + bdot(attn[c], v_new.astype(bf16))
        v_bd = jnp.where(own, jnp.concatenate([v_new] * HEADS, axis=1), 0.0).astype(bf16)
        state = state * elast[c] + bdot(kd[c], v_bd)
        for h in range(HEADS):
            sl = slice(h * HEAD_DIM, (h + 1) * HEAD_DIM)
            zh = z_ref[rows[c], sl]
            o_ref[rows[c], sl] = (_rms(o[h * CHUNK:(h + 1) * CHUNK], gn) * _silu(zh)).astype(bf16)
    s_ref[...] = state


def _gdn(proj, conv_w, alog_row, dtb_row, gn):
    b, tp, _ = proj.shape
    tb = ATT_TILE
    w3 = 3 * GROUP
    return pl.pallas_call(
        functools.partial(_gdn_kernel, tb=tb),
        grid=(b, tp // tb),
        in_specs=[
            pl.BlockSpec((None, tb, w3), lambda bi, t: (bi, t, COL_B // w3)),
            pl.BlockSpec((None, tb, GROUP), lambda bi, t: (bi, t, (COL_B + w3) // GROUP)),
            pl.BlockSpec((None, tb, LANES), lambda bi, t: (bi, t, COL_SM // LANES)),
            pl.BlockSpec((CONV_K, w3), lambda bi, t: (0, 0)),
            pl.BlockSpec((1, LANES), lambda bi, t: (0, 0)),
            pl.BlockSpec((1, LANES), lambda bi, t: (0, 0)),
            pl.BlockSpec((1, HEAD_DIM), lambda bi, t: (0, 0)),
        ],
        out_specs=pl.BlockSpec((None, tb, GROUP), lambda bi, t: (bi, t, 0)),
        out_shape=jax.ShapeDtypeStruct((b, tp, GROUP), bf16),
        scratch_shapes=[pltpu.VMEM((tb + 8, w3), f32), pltpu.VMEM((HEAD_DIM, GROUP), f32)],
        compiler_params=_cparams(("parallel", "arbitrary")),
        name="gdn",
    )(proj, proj, proj, conv_w, alog_row, dtb_row, gn)


def _ret_kernel(qk_ref, v_ref, g_ref, c_ref, sa_ref, sb_ref, dm_ref, xi_ref, zt_ref, gch_ref,
                gn_ref, o_ref, r_ref, *, tc):
    @pl.when(pl.program_id(1) == 0)
    def _():
        r_ref[...] = jnp.zeros_like(r_ref)

    lane = lax.broadcasted_iota(jnp.int32, (tc, LANES), 1)
    c, sa, sb = c_ref[...], sa_ref[...], sb_ref[...]
    gn = gn_ref[...]
    for pair in range(HEADS // 2):
        qp = _rope64(qk_ref[:, pair * LANES:(pair + 1) * LANES], c, sa, sb)
        kp = _rope64(qk_ref[:, 2 * LANES + pair * LANES:2 * LANES + (pair + 1) * LANES],
                     c, sa, sb) * (64 ** -0.5)
        for half in range(2):
            h = 2 * pair + half
            sl = slice(h * HEAD_DIM, (h + 1) * HEAD_DIM)
            own = (lane >= 64) if half else (lane < 64)
            qm = jnp.where(own, qp, 0.0)
            km = jnp.where(own, kp, 0.0)
            vb = v_ref[:, sl].astype(bf16)
            inner = lax.dot_general(qm.astype(bf16), km.astype(bf16), _NT,
                                    preferred_element_type=f32) * dm_ref[h]
            r = r_ref[h]
            o = (jnp.dot(inner.astype(bf16), vb, preferred_element_type=f32)
                 + jnp.dot((qm * xi_ref[h]).astype(bf16), r.astype(bf16),
                           preferred_element_type=f32))
            kz = (km * zt_ref[h]).T.astype(bf16)
            r_ref[h] = r * gch_ref[h] + jnp.dot(kz, vb, preferred_element_type=f32)
            o_ref[:, sl] = (_silu(g_ref[:, sl]) * _rms(o, gn)).astype(bf16)


def _retention(proj, tabs64, consts, gn):
    b, tp, _ = proj.shape
    tc = ATT_TILE
    dmat, xi, zeta, gch = consts
    tab_spec = pl.BlockSpec((tc, LANES), lambda bi, t: (t, 0))
    cb = COL_C // GROUP
    full3 = lambda a: pl.BlockSpec(a.shape, lambda bi, t: (0, 0, 0))
    return pl.pallas_call(
        functools.partial(_ret_kernel, tc=tc),
        grid=(b, tp // tc),
        in_specs=[
            pl.BlockSpec((None, tc, GROUP), lambda bi, t: (bi, t, cb)),
            pl.BlockSpec((None, tc, GROUP), lambda bi, t: (bi, t, cb + 1)),
            pl.BlockSpec((None, tc, GROUP), lambda bi, t: (bi, t, cb + 2)),
            tab_spec, tab_spec, tab_spec,
            full3(dmat), full3(xi), full3(zeta), full3(gch),
            pl.BlockSpec((1, HEAD_DIM), lambda bi, t: (0, 0)),
        ],
        out_specs=pl.BlockSpec((None, tc, GROUP), lambda bi, t: (bi, t, 0)),
        out_shape=jax.ShapeDtypeStruct((b, tp, GROUP), bf16),
        scratch_shapes=[pltpu.VMEM((HEADS, HEAD_DIM, HEAD_DIM), f32)],
        compiler_params=_cparams(("parallel", "arbitrary")),
        name="retention",
    )(proj, proj, proj, *tabs64, dmat, xi, zeta, gch, gn)


def _retention_consts(tc):
    log_g = jnp.log(1.0 - 2.0 ** (-5.0 - jnp.arange(HEADS, dtype=f32)))
    i = jnp.arange(tc, dtype=f32)
    dist = i[:, None] - i[None, :]
    dmat = jnp.where(dist >= 0, jnp.exp(log_g[:, None, None] * jnp.maximum(dist, 0.0)), 0.0)
    ones = jnp.ones((1, 1, LANES), f32)
    xi = jnp.exp(log_g[:, None] * (i + 1.0))[..., None] * ones
    zeta = jnp.exp(log_g[:, None] * (tc - 1.0 - i))[..., None] * ones
    gch = jnp.exp(log_g * tc)[:, None, None] * ones
    return dmat, xi, zeta, gch


def _prep_d_kernel(q_ref, k_ref, v_ref, iq0_ref, iq1_ref, ik_ref, sm_ref, c64_ref, sa_ref, sb_ref,
                   c128_ref, s128_ref, gq_ref, gk_ref,
                   qo_ref, ko_ref, vo_ref, iqo_ref, iqr_ref, iko_ref, iwo_ref):
    c64, sa, sb = c64_ref[...], sa_ref[...], sb_ref[...]
    c128, s128 = c128_ref[...], s128_ref[...]
    for h in range(HEADS):
        sl = slice(h * HEAD_DIM, (h + 1) * HEAD_DIM)
        q = _rope128(_rms(q_ref[:, sl], gq_ref[...]), c128, s128) * (HEAD_DIM ** -0.5)
        qo_ref[h] = q.T.astype(bf16)
        ko_ref[h] = _rope128(_rms(k_ref[:, sl], gk_ref[...]), c128, s128).astype(bf16)
        vo_ref[h] = v_ref[:, sl].T.astype(bf16)
    mw = 2 * LANES
    for p in range(IDX_HEADS // 2):
        src = iq0_ref if p < 4 else iq1_ref
        xt = _rope64(src[:, (p % 4) * LANES:(p % 4 + 1) * LANES], c64, sa, sb).T.astype(bf16)
        iqo_ref[2 * p] = xt[:IDX_DIM, :mw]
        iqo_ref[2 * p + 1] = xt[IDX_DIM:, :mw]
        iqr_ref[p] = jnp.concatenate([xt[:IDX_DIM, mw:], xt[IDX_DIM:, mw:]], axis=1)
    iko_ref[...] = _rope64(ik_ref[...], c64, sa, sb)[:, :IDX_DIM].astype(bf16)
    iwo_ref[...] = sm_ref[...].T[SM_IW:SM_IW + IDX_HEADS]


def _prep_d(proj, tabs64, tabs128, gq, gk):
    b, tp, _ = proj.shape
    tm = ATT_TILE
    nt = tp // tm
    cb = COL_D // GROUP
    tab_spec = pl.BlockSpec((tm, LANES), lambda bi, i: (i, 0))
    g_spec = pl.BlockSpec((1, LANES), lambda bi, i: (0, 0))
    col = lambda k: pl.BlockSpec((None, tm, GROUP), lambda bi, i: (bi, i, cb + k))
    narrow = lambda c0: pl.BlockSpec((None, tm, LANES), lambda bi, i: (bi, i, c0 // LANES))
    tspec = pl.BlockSpec((None, HEADS, None, HEAD_DIM, tm), lambda bi, i: (bi, 0, i, 0, 0))
    tshape = jax.ShapeDtypeStruct((b, HEADS, nt, HEAD_DIM, tm), bf16)
    return pl.pallas_call(
        _prep_d_kernel,
        grid=(b, nt),
        in_specs=[col(0), col(1), col(2), col(3), col(4), narrow(COL_IK), narrow(COL_SM),
                  tab_spec, tab_spec, tab_spec, tab_spec, tab_spec, g_spec, g_spec],
        out_specs=[tspec,
                   pl.BlockSpec((None, HEADS, tm, HEAD_DIM), lambda bi, i: (bi, 0, i, 0)),
                   tspec,
                   pl.BlockSpec((None, None, IDX_HEADS, IDX_DIM, 2 * LANES),
                                lambda bi, i: (bi, i, 0, 0, 0)),
                   pl.BlockSpec((None, None, IDX_HEADS // 2, IDX_DIM, 2 * LANES),
                                lambda bi, i: (bi, i, 0, 0, 0)),
                   pl.BlockSpec((None, tm, IDX_DIM), lambda bi, i: (bi, i, 0)),
                   pl.BlockSpec((None, None, IDX_HEADS, tm), lambda bi, i: (bi, i, 0, 0))],
        out_shape=[tshape,
                   jax.ShapeDtypeStruct((b, HEADS, tp, HEAD_DIM), bf16),
                   tshape,
                   jax.ShapeDtypeStruct((b, nt, IDX_HEADS, IDX_DIM, 2 * LANES), bf16),
                   jax.ShapeDtypeStruct((b, nt, IDX_HEADS // 2, IDX_DIM, 2 * LANES), bf16),
                   jax.ShapeDtypeStruct((b, tp, IDX_DIM), bf16),
                   jax.ShapeDtypeStruct((b, nt, IDX_HEADS, tm), f32)],
        compiler_params=_cparams(("parallel", "parallel")),
        name="prep_d",
    )(proj, proj, proj, proj, proj, proj, proj, *tabs64, *tabs128, gq, gk)


def _key_to_f32(key):
    return lax.bitcast_convert_type(jnp.where(key < 0, key ^ 0x7FFFFFFF, key), f32)


def _dsa_kernel(iq_ref, iqr_ref, iw_ref, ik_ref, q_ref, k_ref, vt_ref, o_ref,
                sc_ref, sc16_ref, cs_ref, *, tq, n_keep):
    i = pl.program_id(1)
    nkb = i + 1
    row = lax.broadcasted_iota(jnp.int32, (tq, tq), 0)
    col = lax.broadcasted_iota(jnp.int32, (tq, tq), 1)
    causal = row <= col
    iw = iw_ref[...]
    kf = float(n_keep)
    int_min = -2 ** 31
    neg_inf_key = int_min + 0x7FFFFF

    mw = 2 * LANES

    def scores(j, masked):
        start = pl.multiple_of(j * tq, tq)
        ikb = ik_ref[pl.ds(start, tq), :]
        sc_a = jnp.zeros((tq, mw), f32)
        for hh in range(IDX_HEADS):
            s = jnp.dot(ikb, iq_ref[hh], preferred_element_type=f32)
            sc_a = sc_a + iw[hh:hh + 1, :mw] * jnp.maximum(s, 0.0)
        sc_b = jnp.zeros((tq, LANES), f32)
        for p in range(IDX_HEADS // 2):
            s = jnp.dot(ikb, iqr_ref[p], preferred_element_type=f32)
            sc_b = (sc_b + iw[2 * p:2 * p + 1, mw:] * jnp.maximum(s[:, :LANES], 0.0)
                    + iw[2 * p + 1:2 * p + 2, mw:] * jnp.maximum(s[:, LANES:], 0.0))
        sc = jnp.concatenate([sc_a, sc_b], axis=1)
        if masked:
            sc = jnp.where(causal, sc, -jnp.inf)
        sc_ref[j] = sc
        bits = lax.bitcast_convert_type(sc, jnp.int32) & -65536
        sc16_ref[j] = lax.bitcast_convert_type(bits, f32).astype(bf16)

    def _scores_body(j, carry):
        scores(j, False)
        return carry

    lax.fori_loop(0, i, _scores_body, 0)
    scores(i, True)

    def count(pred):
        def body(j, acc):
            return acc + jnp.sum(jnp.where(pred(sc_ref[j], j), 1.0, 0.0), axis=0, keepdims=True)
        return lax.fori_loop(0, nkb, body, jnp.zeros((1, tq), f32))

    pack = 16
    one16, zero16 = jnp.ones((), bf16), jnp.zeros((), bf16)

    def count16(c16):
        def body(j, acc):
            ind = jnp.where(sc16_ref[j] >= c16, one16, zero16)
            part = ind[0:pack]
            for t in range(1, tq // pack):
                part = part + ind[t * pack:(t + 1) * pack]
            return acc + jnp.sum(part.astype(f32), axis=0, keepdims=True)
        return lax.fori_loop(0, nkb, body, jnp.zeros((1, tq), f32))

    def bisect16(it, ans):
        cand = ans + lax.shift_left(jnp.int32(1), 31 - it)
        c16 = lax.bitcast_convert_type(
            jnp.where(cand < 0, cand ^ 0x7FFFFFFF, cand) & -65536, f32).astype(bf16)
        return jnp.where(count16(c16) >= kf, cand, ans)

    def bisect(it, ans):
        cand = ans + lax.shift_left(jnp.int32(1), 31 - it)
        cand_f = _key_to_f32(cand)
        cnt = count(lambda blk, j: blk >= cand_f)
        return jnp.where(cnt >= kf, cand, ans)

    ans = lax.fori_loop(0, 16, bisect16, jnp.full((1, tq), int_min, jnp.int32))
    ans = lax.fori_loop(16, 32, bisect, ans)
    thr_key = jnp.maximum(ans, neg_inf_key)
    thr = _key_to_f32(thr_key)

    n_ge = count(lambda blk, j: blk >= thr)
    need = (n_ge > kf) & (thr_key > neg_inf_key)
    big = float(2 ** 30)
    cs_ref[...] = jnp.full((8, tq), big, f32)

    @pl.when(jnp.max(jnp.where(need, 1.0, 0.0)) > 0.0)
    def _():
        budget = kf - count(lambda blk, j: blk > thr)

        def tie_bisect(it, lo):
            cand = lo + lax.shift_left(jnp.int32(1), 14 - it).astype(f32)

            def pred(blk, j):
                keyg = (row + j * tq).astype(f32)
                return jnp.where(blk == thr, keyg, big) < cand
            cnt = count(pred)
            return jnp.where(cnt <= budget - 1.0, cand, lo)

        kstar = lax.fori_loop(0, 15, tie_bisect, jnp.zeros((1, tq), f32))
        cs_ref[...] = jnp.broadcast_to(jnp.where(need, kstar, big), (8, tq))

    kstar = cs_ref[0:1, :]

    def bias(j, masked):
        blk = sc_ref[j]
        keyg = (row + j * tq).astype(f32)
        sel = (blk > thr) | ((blk == thr) & (keyg <= kstar))
        if masked:
            sel = sel & causal
        sc_ref[j] = jnp.where(sel, 0.0, NEG)

    def _bias_body(j, carry):
        bias(j, False)
        return carry

    lax.fori_loop(0, i, _bias_body, 0)
    bias(i, True)

    qs = [q_ref[h] for h in range(HEADS)]

    def scores_att(j):
        start = pl.multiple_of(j * tq, tq)
        b_ = sc_ref[j]
        return tuple(jnp.dot(k_ref[h, pl.ds(start, tq), :], qs[h], preferred_element_type=f32) + b_
                     for h in range(HEADS))

    sm = _flash_blocks(nkb, scores_att, lambda h, j: vt_ref[h, j], None, HEADS, tq)
    for h in range(HEADS):
        l, acc = sm[h]
        o_ref[:, h * HEAD_DIM:(h + 1) * HEAD_DIM] = (acc * (1.0 / l)).T.astype(bf16)


def _dsa(iqt, iqr, iwt, ik, qt, k, vt, n_keep):
    b, _, nq, _, tq = qt.shape
    assert n_keep <= tq and tq == 3 * LANES
    tp = nq * tq
    return pl.pallas_call(
        functools.partial(_dsa_kernel, tq=tq, n_keep=n_keep),
        grid=(b, nq),
        in_specs=[
            pl.BlockSpec((None, None, IDX_HEADS, IDX_DIM, 2 * LANES),
                         lambda bi, i: (bi, i, 0, 0, 0)),
            pl.BlockSpec((None, None, IDX_HEADS // 2, IDX_DIM, 2 * LANES),
                         lambda bi, i: (bi, i, 0, 0, 0)),
            pl.BlockSpec((None, None, IDX_HEADS, tq), lambda bi, i: (bi, i, 0, 0)),
            pl.BlockSpec((None, tp, IDX_DIM), lambda bi, i: (bi, 0, 0)),
            pl.BlockSpec((None, HEADS, None, HEAD_DIM, tq), lambda bi, i: (bi, 0, i, 0, 0)),
            pl.BlockSpec((None, HEADS, tp, HEAD_DIM), lambda bi, i: (bi, 0, 0, 0)),
            pl.BlockSpec((None, HEADS, nq, HEAD_DIM, tq), lambda bi, i: (bi, 0, 0, 0, 0)),
        ],
        out_specs=pl.BlockSpec((None, tq, GROUP), lambda bi, i: (bi, i, 0)),
        out_shape=jax.ShapeDtypeStruct((b, tp, GROUP), bf16),
        scratch_shapes=[pltpu.VMEM((nq, tq, tq), f32), pltpu.VMEM((nq, tq, tq), bf16),
                        pltpu.VMEM((8, tq), f32)],
        compiler_params=_cparams(("parallel", "arbitrary")),
        name="dsa",
    )(iqt, iqr, iwt, ik, qt, k, vt)


def _rope_tables(tp):
    pos = jnp.arange(tp, dtype=f32)

    def cs(d):
        inv = ROPE_THETA ** (-jnp.arange(0, d, 2, dtype=f32) / d)
        ang = pos[:, None] * inv[None, :]
        return jnp.cos(ang), jnp.sin(ang)

    c, s = cs(64)
    z = jnp.zeros_like(s)
    tabs64 = (jnp.tile(jnp.concatenate([c, c], axis=1), (1, 2)),
              jnp.tile(jnp.concatenate([-s, z], axis=1), (1, 2)),
              jnp.tile(jnp.concatenate([z, s], axis=1), (1, 2)))
    c, s = cs(128)
    tabs128 = (jnp.concatenate([c, c], axis=1), jnp.concatenate([-s, s], axis=1))
    return tabs64, tabs128


def _permute_w_in(w):
    z = lambda n: jnp.zeros(w.shape[:-1] + (n,), w.dtype)
    o_beta = 7 * GROUP
    o_c = o_beta + 2 * HEADS
    o_ik = o_c + 8 * GROUP
    o_iw = o_ik + IDX_DIM
    end = o_iw + IDX_HEADS
    return jnp.concatenate([
        w[..., :o_beta], w[..., o_c:o_ik], w[..., o_ik:o_iw], z(LANES - IDX_DIM),
        w[..., o_iw:end], w[..., o_beta:o_c], z(NP - COL_SM - IDX_HEADS - 2 * HEADS)], axis=-1)


def _lane_row(v, offset):
    return jnp.zeros((1, LANES), f32).at[0, offset:offset + v.shape[0]].set(v.astype(f32))


def kernel(x, meta_tokens, ffn1_norm, ffn1_w_in, ffn1_w_out, mix_norm, w_in, w_out, diff_q_norm, diff_k_norm, diff_lambda_q1, diff_lambda_k1, diff_lambda_q2, diff_lambda_k2, diff_out_norm, gdn_conv_w, gdn_a_log, gdn_dt_bias, gdn_out_norm, ret_out_norm, dsa_q_norm, dsa_k_norm, ffn2_norm, ffn2_w_in, ffn2_w_out):
    b, seq, d = x.shape
    depth = w_in.shape[0]
    n_keep = min(TOPK_MAX, seq // 4)
    t = seq + N_META
    tp = -(-t // ATT_TILE) * ATT_TILE
    rows = b * tp

    h = jnp.concatenate([jnp.broadcast_to(meta_tokens.astype(x.dtype)[None], (b, N_META, d)), x,
                         jnp.zeros((b, tp - t, d), x.dtype)], axis=1).reshape(rows, d)
    tabs64, tabs128 = _rope_tables(tp)
    ret_consts = _retention_consts(ATT_TILE)
    row = lambda v: v.astype(f32)[None, :]
    twice = lambda v: jnp.concatenate([v, v]).astype(f32)[None, :]

    gains = lambda g: g.astype(f32)[:, None, :]
    ffn1 = (gains(ffn1_norm), ffn1_w_in.astype(bf16), ffn1_w_out.astype(bf16))
    ffn2 = (gains(ffn2_norm), ffn2_w_in.astype(bf16), ffn2_w_out.astype(bf16))
    w_in_p = _permute_w_in(w_in).astype(bf16)
    w_out_b = w_out.astype(bf16)
    mix_g = gains(mix_norm)

    for l in range(depth):
        lambda_init = 0.8 - 0.6 * math.exp(-0.3 * l)
        h = _ffn(h, *ffn1, l)
        proj = _proj(h, mix_g, w_in_p, l).reshape(b, tp, NP)

        qz, ka, va = _prep_a(proj, tabs64, twice(diff_q_norm[l]), twice(diff_k_norm[l]))
        y_a = _attn_a(qz, ka, va, row(diff_lambda_q1[l]), row(diff_lambda_k1[l]),
                      row(diff_lambda_q2[l]), row(diff_lambda_k2[l]), row(diff_out_norm[l]),
                      lambda_init)
        y_b = _gdn(proj, gdn_conv_w[l].astype(f32), _lane_row(gdn_a_log[l], SM_ALPHA),
                   _lane_row(gdn_dt_bias[l], SM_ALPHA), row(gdn_out_norm[l]))
        y_c = _retention(proj, tabs64, ret_consts, row(ret_out_norm[l]))
        qd, kd, vd, iq, iqr, ik, iw = _prep_d(proj, tabs64, tabs128, row(dsa_q_norm[l]),
                                              row(dsa_k_norm[l]))
        y_d = _dsa(iq, iqr, iw, ik, qd, kd, vd, n_keep)

        ys = [y.reshape(rows, GROUP) for y in (y_a, y_b, y_c, y_d)]
        h = _outproj(h, ys, w_out_b, l)
        h = _ffn(h, *ffn2, l)
    return h.reshape(b, tp, d)[:, N_META:t]
```

```python
import functools
import math

import jax
import jax.numpy as jnp
from jax import lax
from jax.experimental import pallas as pl
from jax.experimental.pallas import tpu as pltpu

f32 = jnp.float32
bf16 = jnp.bfloat16

N_META = 16
HEAD_DIM = 128
HEADS = 4
GROUP = HEADS * HEAD_DIM
EPS = 1e-6
ROPE_THETA = 10000.0
NEG = -1e30
TOPK_MAX = 256
CONV_K = 4
IDX_HEADS = 16
IDX_DIM = 64
CHUNK = 64
LANES = 128
ATT_TILE = 384
VMEM_LIMIT = 56 * 1024 * 1024

NP = 8192
COL_A = 0
COL_B = 1536
COL_C = 3584
COL_D = 5120
COL_IK = 7680
COL_SM = 7808
SM_IW, SM_BETA, SM_ALPHA = 0, 16, 20

_NT = (((1,), (1,)), ((), ()))


def _cparams(sem):
    return pltpu.CompilerParams(dimension_semantics=sem, vmem_limit_bytes=VMEM_LIMIT)


def _row_tile(rows, target):
    best = 8
    for t in range(8, target + 1, 8):
        if rows % t == 0:
            best = t
    return best


def _silu(x):
    return x * jax.nn.sigmoid(x)


def _rms(x, g):
    return x * lax.rsqrt(jnp.mean(x * x, axis=-1, keepdims=True) + EPS) * g


def _ffn_kernel(h_ref, g_ref, wg_ref, wu_ref, wo_ref, o_ref, xn_ref):
    @pl.when(pl.program_id(1) == 0)
    def _():
        x = h_ref[...]
        xn_ref[...] = _rms(x, g_ref[...]).astype(bf16)
        o_ref[...] = x

    xn = xn_ref[...]
    gate = jnp.dot(xn, wg_ref[...], preferred_element_type=f32)
    up = jnp.dot(xn, wu_ref[...], preferred_element_type=f32)
    act = (_silu(gate) * up).astype(bf16)
    o_ref[...] += 0.5 * jnp.dot(act, wo_ref[...], preferred_element_type=f32)


def _ffn(h, g, w_in, w_out, l):
    rows, d = h.shape
    ff = w_out.shape[1]
    tm = _row_tile(rows, 768)
    tf = 512
    nf = ff // tf
    return pl.pallas_call(
        _ffn_kernel,
        grid=(rows // tm, nf),
        in_specs=[
            pl.BlockSpec((tm, d), lambda i, j: (i, 0)),
            pl.BlockSpec((None, 1, d), lambda i, j: (l, 0, 0)),
            pl.BlockSpec((None, d, tf), lambda i, j: (l, 0, j)),
            pl.BlockSpec((None, d, tf), lambda i, j: (l, 0, j + nf)),
            pl.BlockSpec((None, tf, d), lambda i, j: (l, j, 0)),
        ],
        out_specs=pl.BlockSpec((tm, d), lambda i, j: (i, 0)),
        out_shape=jax.ShapeDtypeStruct((rows, d), f32),
        scratch_shapes=[pltpu.VMEM((tm, d), bf16)],
        compiler_params=_cparams(("parallel", "arbitrary")),
        name="ffn",
    )(h, g, w_in, w_in, w_out)


def _proj_kernel(h_ref, g_ref, w_ref, o_ref, xn_ref):
    @pl.when(pl.program_id(1) == 0)
    def _():
        xn_ref[...] = _rms(h_ref[...], g_ref[...]).astype(bf16)

    o_ref[...] = jnp.dot(xn_ref[...], w_ref[...], preferred_element_type=f32)


def _proj(h, g, w, l):
    rows, d = h.shape
    n = w.shape[2]
    tm = _row_tile(rows, 768)
    tn = 2048
    return pl.pallas_call(
        _proj_kernel,
        grid=(rows // tm, n // tn),
        in_specs=[
            pl.BlockSpec((tm, d), lambda i, j: (i, 0)),
            pl.BlockSpec((None, 1, d), lambda i, j: (l, 0, 0)),
            pl.BlockSpec((None, d, tn), lambda i, j: (l, 0, j)),
        ],
        out_specs=pl.BlockSpec((tm, tn), lambda i, j: (i, j)),
        out_shape=jax.ShapeDtypeStruct((rows, n), f32),
        scratch_shapes=[pltpu.VMEM((tm, d), bf16)],
        compiler_params=_cparams(("parallel", "arbitrary")),
        name="proj",
    )(h, g, w)


def _outproj_kernel(h_ref, ya_ref, yb_ref, yc_ref, yd_ref, w_ref, o_ref):
    acc = h_ref[...]
    for n, y_ref in enumerate((ya_ref, yb_ref, yc_ref, yd_ref)):
        acc = acc + jnp.dot(y_ref[...], w_ref[n * GROUP:(n + 1) * GROUP, :],
                            preferred_element_type=f32)
    o_ref[...] = acc


def _outproj(h, ys, w, l):
    rows, d = h.shape
    tm = _row_tile(rows, 768)
    y_spec = pl.BlockSpec((tm, GROUP), lambda i: (i, 0))
    return pl.pallas_call(
        _outproj_kernel,
        grid=(rows // tm,),
        in_specs=[pl.BlockSpec((tm, d), lambda i: (i, 0)), y_spec, y_spec, y_spec, y_spec,
                  pl.BlockSpec((None, w.shape[1], d), lambda i: (l, 0, 0))],
        out_specs=pl.BlockSpec((tm, d), lambda i: (i, 0)),
        out_shape=jax.ShapeDtypeStruct((rows, d), f32),
        compiler_params=_cparams(("parallel",)),
        name="outproj",
    )(h, *ys, w)


def _rope64(x, c, sa, sb):
    return x * c + pltpu.roll(x, LANES - 32, 1) * sa + pltpu.roll(x, 32, 1) * sb


def _rope128(x, c, s):
    return x * c + pltpu.roll(x, 64, 1) * s


def _group_mean_matrix():
    r = lax.broadcasted_iota(jnp.int32, (LANES, LANES), 0) >> 6
    c = lax.broadcasted_iota(jnp.int32, (LANES, LANES), 1) >> 6
    return jnp.where(r == c, 1.0 / 64.0, 0.0).astype(bf16)


def _prep_a_kernel(q_ref, k_ref, v_ref, c_ref, sa_ref, sb_ref, gq_ref, gk_ref,
                   qz_ref, ko_ref, vo_ref):
    tm = q_ref.shape[0]
    lane = lax.broadcasted_iota(jnp.int32, (tm, LANES), 1)
    gmat = _group_mean_matrix()
    c, sa, sb = c_ref[...], sa_ref[...], sb_ref[...]

    def norm_rope(x, g):
        xx = x * x
        hi = xx.astype(bf16)
        lo = (xx - hi.astype(f32)).astype(bf16)
        r = jnp.dot(jnp.concatenate([hi, lo], axis=0), gmat, preferred_element_type=f32)
        ms = r[:tm] + r[tm:]
        return _rope64(x * lax.rsqrt(ms + EPS) * g, c, sa, sb)

    for h in range(HEADS):
        sl = slice(h * HEAD_DIM, (h + 1) * HEAD_DIM)
        q = norm_rope(q_ref[:, sl], gq_ref[...]) * (64 ** -0.5)
        qz_ref[h, :, 0:tm] = jnp.where(lane < 64, q, 0.0).T.astype(bf16)
        qz_ref[h, :, tm:2 * tm] = jnp.where(lane >= 64, q, 0.0).T.astype(bf16)
        ko_ref[h] = norm_rope(k_ref[:, sl], gk_ref[...]).astype(bf16)
        vo_ref[h] = v_ref[:, sl].T.astype(bf16)


def _prep_a(proj, tabs64, gq, gk):
    b, tp, _ = proj.shape
    tm = ATT_TILE
    nt = tp // tm
    cb = COL_A // GROUP
    tab_spec = pl.BlockSpec((tm, LANES), lambda bi, i: (i, 0))
    g_spec = pl.BlockSpec((1, LANES), lambda bi, i: (0, 0))
    return pl.pallas_call(
        _prep_a_kernel,
        grid=(b, nt),
        in_specs=[
            pl.BlockSpec((None, tm, GROUP), lambda bi, i: (bi, i, cb)),
            pl.BlockSpec((None, tm, GROUP), lambda bi, i: (bi, i, cb + 1)),
            pl.BlockSpec((None, tm, GROUP), lambda bi, i: (bi, i, cb + 2)),
            tab_spec, tab_spec, tab_spec, g_spec, g_spec,
        ],
        out_specs=[
            pl.BlockSpec((None, HEADS, None, HEAD_DIM, 2 * tm), lambda bi, i: (bi, 0, i, 0, 0)),
            pl.BlockSpec((None, HEADS, tm, HEAD_DIM), lambda bi, i: (bi, 0, i, 0)),
            pl.BlockSpec((None, HEADS, None, HEAD_DIM, tm), lambda bi, i: (bi, 0, i, 0, 0)),
        ],
        out_shape=[
            jax.ShapeDtypeStruct((b, HEADS, nt, HEAD_DIM, 2 * tm), bf16),
            jax.ShapeDtypeStruct((b, HEADS, tp, HEAD_DIM), bf16),
            jax.ShapeDtypeStruct((b, HEADS, nt, HEAD_DIM, tm), bf16),
        ],
        compiler_params=_cparams(("parallel", "parallel")),
        name="prep_a",
    )(proj, proj, proj, *tabs64, gq, gk)


def _flash_blocks(nblk, qk, vt, mask_last, groups, width, unroll):
    gs = range(groups)

    def update(js, c, mask=None):
        tiles = [qk(j) for j in js]
        if mask is not None:
            tiles[-1] = mask(tiles[-1])
        out = []
        for g in gs:
            m, l, acc = c[g]
            m_new = m
            for t in tiles:
                m_new = jnp.maximum(m_new, jnp.max(t[g], axis=0, keepdims=True))
            alpha = jnp.exp(m - m_new)
            l = alpha * l
            ps = []
            for t in tiles:
                p = jnp.exp(t[g] - m_new)
                l = l + jnp.sum(p, axis=0, keepdims=True)
                ps.append(p.astype(bf16))
            vts = [vt(g, j) for j in js]
            pv = jnp.dot(vts[0] if len(js) == 1 else jnp.concatenate(vts, axis=1),
                         ps[0] if len(js) == 1 else jnp.concatenate(ps, axis=0),
                         preferred_element_type=f32)
            out.append((m_new, l, alpha * acc + pv))
        return tuple(out)

    init = tuple((jnp.full((1, width), NEG, f32), jnp.zeros((1, width), f32),
                  jnp.zeros((HEAD_DIM, width), f32)) for _ in gs)
    last = nblk - 1
    c = lax.fori_loop(0, last // unroll,
                      lambda t, c: update([unroll * t + u for u in range(unroll)], c), init)
    rest = [functools.partial(lambda r, c: update([last - r + u for u in range(r + 1)], c,
                                                  mask_last), r) for r in range(unroll)]
    c = lax.switch(last % unroll, rest, c)
    return tuple((l, acc) for _, l, acc in c)


def _attn_a_kernel(qz_ref, k_ref, vt_ref, lq1_ref, lk1_ref, lq2_ref, lk2_ref, gn_ref, o_ref,
                   *, tq, lambda_init):
    i = pl.program_id(2)
    qz = qz_ref[...]
    row = lax.broadcasted_iota(jnp.int32, (tq, 2 * tq), 0)
    col = lax.broadcasted_iota(jnp.int32, (tq, 2 * tq), 1)
    causal = row <= jnp.where(col >= tq, col - tq, col)

    gw = 2 * LANES
    ng = 2 * tq // gw

    def scores(j):
        start = pl.multiple_of(j * tq, tq)
        kb = k_ref[pl.ds(start, tq), :]
        return tuple(jnp.dot(kb, qz[:, g * gw:(g + 1) * gw], preferred_element_type=f32)
                     for g in range(ng))

    def mask_diag(ss):
        return tuple(jnp.where(causal[:, g * gw:(g + 1) * gw], ss[g], NEG) for g in range(ng))

    sm = _flash_blocks(i + 1, scores, lambda g, j: vt_ref[j], mask_diag, ng, gw, unroll=4)
    o = jnp.concatenate([acc * (1.0 / l) for l, acc in sm], axis=1)
    lam = (jnp.exp(jnp.sum(lq1_ref[...] * lk1_ref[...], axis=-1, keepdims=True))
           - jnp.exp(jnp.sum(lq2_ref[...] * lk2_ref[...], axis=-1, keepdims=True))
           + lambda_init)
    od = o[:, :tq] - lam * o[:, tq:]
    odn = od * lax.rsqrt(jnp.mean(od * od, axis=0, keepdims=True) + EPS)
    o_ref[...] = (odn.T * gn_ref[...] * (1.0 - lambda_init)).astype(bf16)


def _attn_a(qz, k, vt, lq1, lk1, lq2, lk2, gn, lambda_init):
    b, _, nt, _, _ = qz.shape
    tq = ATT_TILE
    tp = nt * tq
    l_spec = pl.BlockSpec((1, 64), lambda bi, h, i: (0, 0))
    return pl.pallas_call(
        functools.partial(_attn_a_kernel, tq=tq, lambda_init=lambda_init),
        grid=(b, HEADS, nt),
        in_specs=[
            pl.BlockSpec((None, None, None, HEAD_DIM, 2 * tq), lambda bi, h, i: (bi, h, i, 0, 0)),
            pl.BlockSpec((None, None, tp, HEAD_DIM), lambda bi, h, i: (bi, h, 0, 0)),
            pl.BlockSpec((None, None, nt, HEAD_DIM, tq), lambda bi, h, i: (bi, h, 0, 0, 0)),
            l_spec, l_spec, l_spec, l_spec,
            pl.BlockSpec((1, HEAD_DIM), lambda bi, h, i: (0, 0)),
        ],
        out_specs=pl.BlockSpec((None, tq, HEAD_DIM), lambda bi, h, i: (bi, i, h)),
        out_shape=jax.ShapeDtypeStruct((b, tp, GROUP), bf16),
        compiler_params=_cparams(("parallel", "parallel", "arbitrary")),
        name="attn_a",
    )(qz, k, vt, lq1, lk1, lq2, lk2, gn)


def _gdn_kernel(qkv_ref, z_ref, sm_ref, cw_ref, alog_ref, dtb_ref, gn_ref, o_ref,
                xx_ref, s_ref, *, tb):
    w3 = 3 * GROUP

    @pl.when(pl.program_id(1) == 0)
    def _():
        xx_ref[0:8, :] = jnp.zeros((8, w3), f32)
        s_ref[...] = jnp.zeros_like(s_ref)

    x = qkv_ref[...]
    xx_ref[8:tb + 8, :] = x
    cw = cw_ref[...]
    y = x * cw[CONV_K - 1:CONV_K, :]
    for s in range(1, CONV_K):
        y = y + xx_ref[pl.ds(8 - s, tb), :] * cw[CONV_K - 1 - s:CONV_K - s, :]
    xx_ref[0:8, :] = x[tb - 8:, :]
    y = _silu(y)

    sm = sm_ref[...]
    beta_all = jax.nn.sigmoid(sm)
    g_all = -jnp.exp(alog_ref[...]) * jax.nn.softplus(sm + dtb_ref[...])

    r = HEADS * CHUNK
    ri = lax.broadcasted_iota(jnp.int32, (r, r), 0)
    ci = lax.broadcasted_iota(jnp.int32, (r, r), 1)
    same = (ri >> 6) == (ci >> 6)
    tri = same & (ri >= ci)
    strict = same & (ri > ci)
    lmat = jnp.where(tri, 1.0, 0.0).astype(bf16)
    eye = jnp.where(ri == ci, 1.0, 0.0).astype(f32)
    own = ((lax.broadcasted_iota(jnp.int32, (r, GROUP), 0) >> 6)
           == (lax.broadcasted_iota(jnp.int32, (r, GROUP), 1) >> 7))
    gn = gn_ref[...]

    def bdot(a, b):
        return jnp.dot(a, b, preferred_element_type=f32)

    def split2(x):
        hi = x.astype(bf16)
        return hi, (x - hi.astype(f32)).astype(bf16)

    def split3(x):
        hi = x.astype(bf16)
        r1 = x - hi.astype(f32)
        mid = r1.astype(bf16)
        return hi, mid, (r1 - mid.astype(f32)).astype(bf16)

    def stack(fn):
        return jnp.concatenate([fn(h) for h in range(HEADS)], axis=0)

    def diag_blocks(x):
        return stack(lambda h: x[h * CHUNK:(h + 1) * CHUNK, h * HEAD_DIM:(h + 1) * HEAD_DIM])

    nc = tb // CHUNK
    chunks = range(nc)
    rows = [slice(c * CHUNK, (c + 1) * CHUNK) for c in chunks]

    def heads_of(c, base):
        return stack(lambda h: y[rows[c], base + h * HEAD_DIM:base + (h + 1) * HEAD_DIM])

    def gate_of(c, vals, lane0):
        return stack(lambda h: jnp.broadcast_to(vals[rows[c], lane0 + h:lane0 + h + 1],
                                                (CHUNK, LANES)))

    qs = [heads_of(c, 0) for c in chunks]
    ks = [heads_of(c, GROUP) for c in chunks]
    vs = [heads_of(c, 2 * GROUP) for c in chunks]
    qs = [q * lax.rsqrt(jnp.sum(q * q, axis=-1, keepdims=True) + EPS) * (HEAD_DIM ** -0.5)
          for q in qs]
    ks = [k * lax.rsqrt(jnp.sum(k * k, axis=-1, keepdims=True) + EPS) for k in ks]
    bs = [gate_of(c, beta_all, SM_BETA) for c in chunks]
    gs = [gate_of(c, g_all, SM_ALPHA) for c in chunks]
    gy = [bdot(lmat, jnp.concatenate(
        split3(jnp.where(strict, jnp.concatenate([g, g], axis=1), 0.0)) + split3(g), axis=1))
        for g in gs]
    diff = [(t[:, 2 * r:3 * r] + t[:, r:2 * r]) + t[:, :r] for t in gy]
    gcum = [(t[:, 3 * r + 2 * LANES:] + t[:, 3 * r + LANES:3 * r + 2 * LANES])
            + t[:, 3 * r:3 * r + LANES] for t in gy]
    decay = [jnp.where(tri, jnp.exp(t), 0.0) for t in diff]
    egc = [jnp.exp(t) for t in gcum]
    kb = [k * b_ for k, b_ in zip(ks, bs)]
    ksb = [k.astype(bf16) for k in ks]
    a_mat = [jnp.where(strict, lax.dot_general(kb[c].astype(bf16), ksb[c], _NT,
                                               preferred_element_type=f32) * decay[c], 0.0)
             for c in chunks]
    t_inv = [eye - a for a in a_mat]
    a_sp = [split2(a) for a in a_mat]
    r1 = [bdot(jnp.concatenate([a_h, a_l], axis=0), a_h) for a_h, a_l in a_sp]
    r2 = [bdot(a_h, a_l) for a_h, a_l in a_sp]
    pw = [(r1[c][r:] + r2[c]) + r1[c][:r] for c in chunks]
    n_sq = int(math.log2(CHUNK)) - 1
    for it in range(n_sq):
        p_sp = [split2(p) for p in pw]
        t_sp = [split2(t) for t in t_inv]
        if it + 1 < n_sq:
            r1 = [bdot(jnp.concatenate([t_sp[c][0], p_sp[c][0], t_sp[c][1], p_sp[c][1]], axis=0),
                       p_sp[c][0]) for c in chunks]
            r2 = [bdot(jnp.concatenate([t_sp[c][0], p_sp[c][0]], axis=0), p_sp[c][1])
                  for c in chunks]
            t_inv = [t_inv[c] + ((r1[c][2 * r:3 * r] + r2[c][:r]) + r1[c][:r]) for c in chunks]
            pw = [(r1[c][3 * r:] + r2[c][r:]) + r1[c][r:2 * r] for c in chunks]
        else:
            r1 = [bdot(jnp.concatenate([t_sp[c][0], t_sp[c][1]], axis=0), p_sp[c][0])
                  for c in chunks]
            r2 = [bdot(t_sp[c][0], p_sp[c][1]) for c in chunks]
            t_inv = [t_inv[c] + ((r1[c][r:] + r2[c]) + r1[c][:r]) for c in chunks]
    uw = [bdot(t_inv[c].astype(bf16),
               jnp.concatenate([vs[c] * bs[c], kb[c] * egc[c]], axis=1).astype(bf16))
          for c in chunks]
    attn = [(lax.dot_general(qs[c].astype(bf16), ksb[c], _NT, preferred_element_type=f32)
             * decay[c]).astype(bf16) for c in chunks]
    qg = [(qs[c] * egc[c]).astype(bf16) for c in chunks]
    last = lambda t, h: t[(h + 1) * CHUNK - 1:(h + 1) * CHUNK, :]
    kd = [(ks[c] * jnp.exp(stack(lambda h: jnp.broadcast_to(last(gcum[c], h), (CHUNK, LANES)))
                           - gcum[c])).T.astype(bf16) for c in chunks]
    elast = [jnp.concatenate([jnp.exp(last(gcum[c], h)) for h in range(HEADS)], axis=1)
             for c in chunks]

    state = s_ref[...]
    for c in chunks:
        u, w = uw[c][:, :HEAD_DIM], uw[c][:, HEAD_DIM:]
        sb = state.astype(bf16)
        v_new = u - diag_blocks(bdot(w.astype(bf16), sb))
        o = diag_blocks(bdot(qg[c], sb)) + bdot(attn[c], v_new.astype(bf16))
        v_bd = jnp.where(own, jnp.concatenate([v_new] * HEADS, axis=1), 0.0).astype(bf16)
        state = state * elast[c] + bdot(kd[c], v_bd)
        for h in range(HEADS):
            sl = slice(h * HEAD_DIM, (h + 1) * HEAD_DIM)
            zh = z_ref[rows[c], sl]
            o_ref[rows[c], sl] = (_rms(o[h * CHUNK:(h + 1) * CHUNK], gn) * _silu(zh)).astype(bf16)
    s_ref[...] = state


def _gdn(proj, conv_w, alog_row, dtb_row, gn):
    b, tp, _ = proj.shape
    tb = ATT_TILE
    w3 = 3 * GROUP
    return pl.pallas_call(
        functools.partial(_gdn_kernel, tb=tb),
        grid=(b, tp // tb),
        in_specs=[
            pl.BlockSpec((None, tb, w3), lambda bi, t: (bi, t, COL_B // w3)),
            pl.BlockSpec((None, tb, GROUP), lambda bi, t: (bi, t, (COL_B + w3) // GROUP)),
            pl.BlockSpec((None, tb, LANES), lambda bi, t: (bi, t, COL_SM // LANES)),
            pl.BlockSpec((CONV_K, w3), lambda bi, t: (0, 0)),
            pl.BlockSpec((1, LANES), lambda bi, t: (0, 0)),
            pl.BlockSpec((1, LANES), lambda bi, t: (0, 0)),
            pl.BlockSpec((1, HEAD_DIM), lambda bi, t: (0, 0)),
        ],
        out_specs=pl.BlockSpec((None, tb, GROUP), lambda bi, t: (bi, t, 0)),
        out_shape=jax.ShapeDtypeStruct((b, tp, GROUP), bf16),
        scratch_shapes=[pltpu.VMEM((tb + 8, w3), f32), pltpu.VMEM((HEAD_DIM, GROUP), f32)],
        compiler_params=_cparams(("parallel", "arbitrary")),
        name="gdn",
    )(proj, proj, proj, conv_w, alog_row, dtb_row, gn)


def _ret_kernel(qk_ref, v_ref, g_ref, c_ref, sa_ref, sb_ref, dm_ref, xi_ref, zt_ref, gch_ref,
                gn_ref, o_ref, r_ref, *, tc):
    @pl.when(pl.program_id(1) == 0)
    def _():
        r_ref[...] = jnp.zeros_like(r_ref)

    lane = lax.broadcasted_iota(jnp.int32, (tc, LANES), 1)
    c, sa, sb = c_ref[...], sa_ref[...], sb_ref[...]
    gn = gn_ref[...]
    for pair in range(HEADS // 2):
        qp = _rope64(qk_ref[:, pair * LANES:(pair + 1) * LANES], c, sa, sb)
        kp = _rope64(qk_ref[:, 2 * LANES + pair * LANES:2 * LANES + (pair + 1) * LANES],
                     c, sa, sb) * (64 ** -0.5)
        for half in range(2):
            h = 2 * pair + half
            sl = slice(h * HEAD_DIM, (h + 1) * HEAD_DIM)
            own = (lane >= 64) if half else (lane < 64)
            qm = jnp.where(own, qp, 0.0)
            km = jnp.where(own, kp, 0.0)
            vb = v_ref[:, sl].astype(bf16)
            inner = lax.dot_general(qm.astype(bf16), km.astype(bf16), _NT,
                                    preferred_element_type=f32) * dm_ref[h]
            r = r_ref[h]
            o = (jnp.dot(inner.astype(bf16), vb, preferred_element_type=f32)
                 + jnp.dot((qm * xi_ref[h]).astype(bf16), r.astype(bf16),
                           preferred_element_type=f32))
            kz = (km * zt_ref[h]).T.astype(bf16)
            r_ref[h] = r * gch_ref[h] + jnp.dot(kz, vb, preferred_element_type=f32)
            o_ref[:, sl] = (_silu(g_ref[:, sl]) * _rms(o, gn)).astype(bf16)


def _retention(proj, tabs64, consts, gn):
    b, tp, _ = proj.shape
    tc = ATT_TILE
    dmat, xi, zeta, gch = consts
    tab_spec = pl.BlockSpec((tc, LANES), lambda bi, t: (t, 0))
    cb = COL_C // GROUP
    full3 = lambda a: pl.BlockSpec(a.shape, lambda bi, t: (0, 0, 0))
    return pl.pallas_call(
        functools.partial(_ret_kernel, tc=tc),
        grid=(b, tp // tc),
        in_specs=[
            pl.BlockSpec((None, tc, GROUP), lambda bi, t: (bi, t, cb)),
            pl.BlockSpec((None, tc, GROUP), lambda bi, t: (bi, t, cb + 1)),
            pl.BlockSpec((None, tc, GROUP), lambda bi, t: (bi, t, cb + 2)),
            tab_spec, tab_spec, tab_spec,
            full3(dmat), full3(xi), full3(zeta), full3(gch),
            pl.BlockSpec((1, HEAD_DIM), lambda bi, t: (0, 0)),
        ],
        out_specs=pl.BlockSpec((None, tc, GROUP), lambda bi, t: (bi, t, 0)),
        out_shape=jax.ShapeDtypeStruct((b, tp, GROUP), bf16),
        scratch_shapes=[pltpu.VMEM((HEADS, HEAD_DIM, HEAD_DIM), f32)],
        compiler_params=_cparams(("parallel", "arbitrary")),
        name="retention",
    )(proj, proj, proj, *tabs64, dmat, xi, zeta, gch, gn)


def _retention_consts(tc):
    log_g = jnp.log(1.0 - 2.0 ** (-5.0 - jnp.arange(HEADS, dtype=f32)))
    i = jnp.arange(tc, dtype=f32)
    dist = i[:, None] - i[None, :]
    dmat = jnp.where(dist >= 0, jnp.exp(log_g[:, None, None] * jnp.maximum(dist, 0.0)), 0.0)
    ones = jnp.ones((1, 1, LANES), f32)
    xi = jnp.exp(log_g[:, None] * (i + 1.0))[..., None] * ones
    zeta = jnp.exp(log_g[:, None] * (tc - 1.0 - i))[..., None] * ones
    gch = jnp.exp(log_g * tc)[:, None, None] * ones
    return dmat, xi, zeta, gch


def _prep_d_kernel(q_ref, k_ref, v_ref, iq0_ref, iq1_ref, ik_ref, sm_ref, c64_ref, sa_ref, sb_ref,
                   c128_ref, s128_ref, gq_ref, gk_ref,
                   qo_ref, ko_ref, vo_ref, iqo_ref, iko_ref, iwo_ref):
    c64, sa, sb = c64_ref[...], sa_ref[...], sb_ref[...]
    c128, s128 = c128_ref[...], s128_ref[...]
    for h in range(HEADS):
        sl = slice(h * HEAD_DIM, (h + 1) * HEAD_DIM)
        q = _rope128(_rms(q_ref[:, sl], gq_ref[...]), c128, s128) * (HEAD_DIM ** -0.5)
        qo_ref[h] = q.T.astype(bf16)
        ko_ref[h] = _rope128(_rms(k_ref[:, sl], gk_ref[...]), c128, s128).astype(bf16)
        vo_ref[h] = v_ref[:, sl].T.astype(bf16)
    for p in range(IDX_HEADS // 2):
        src = iq0_ref if p < 4 else iq1_ref
        xt = _rope64(src[:, (p % 4) * LANES:(p % 4 + 1) * LANES], c64, sa, sb).T.astype(bf16)
        iqo_ref[2 * p] = xt[:IDX_DIM]
        iqo_ref[2 * p + 1] = xt[IDX_DIM:]
    iko_ref[...] = _rope64(ik_ref[...], c64, sa, sb)[:, :IDX_DIM].astype(bf16)
    iwo_ref[...] = sm_ref[...].T[SM_IW:SM_IW + IDX_HEADS]


def _prep_d(proj, tabs64, tabs128, gq, gk):
    b, tp, _ = proj.shape
    tm = ATT_TILE
    nt = tp // tm
    cb = COL_D // GROUP
    tab_spec = pl.BlockSpec((tm, LANES), lambda bi, i: (i, 0))
    g_spec = pl.BlockSpec((1, LANES), lambda bi, i: (0, 0))
    col = lambda k: pl.BlockSpec((None, tm, GROUP), lambda bi, i: (bi, i, cb + k))
    narrow = lambda c0: pl.BlockSpec((None, tm, LANES), lambda bi, i: (bi, i, c0 // LANES))
    tspec = pl.BlockSpec((None, HEADS, None, HEAD_DIM, tm), lambda bi, i: (bi, 0, i, 0, 0))
    tshape = jax.ShapeDtypeStruct((b, HEADS, nt, HEAD_DIM, tm), bf16)
    return pl.pallas_call(
        _prep_d_kernel,
        grid=(b, nt),
        in_specs=[col(0), col(1), col(2), col(3), col(4), narrow(COL_IK), narrow(COL_SM),
                  tab_spec, tab_spec, tab_spec, tab_spec, tab_spec, g_spec, g_spec],
        out_specs=[tspec,
                   pl.BlockSpec((None, HEADS, tm, HEAD_DIM), lambda bi, i: (bi, 0, i, 0)),
                   tspec,
                   pl.BlockSpec((None, None, IDX_HEADS, IDX_DIM, tm), lambda bi, i: (bi, i, 0, 0, 0)),
                   pl.BlockSpec((None, tm, IDX_DIM), lambda bi, i: (bi, i, 0)),
                   pl.BlockSpec((None, None, IDX_HEADS, tm), lambda bi, i: (bi, i, 0, 0))],
        out_shape=[tshape,
                   jax.ShapeDtypeStruct((b, HEADS, tp, HEAD_DIM), bf16),
                   tshape,
                   jax.ShapeDtypeStruct((b, nt, IDX_HEADS, IDX_DIM, tm), bf16),
                   jax.ShapeDtypeStruct((b, tp, IDX_DIM), bf16),
                   jax.ShapeDtypeStruct((b, nt, IDX_HEADS, tm), f32)],
        compiler_params=_cparams(("parallel", "parallel")),
        name="prep_d",
    )(proj, proj, proj, proj, proj, proj, proj, *tabs64, *tabs128, gq, gk)


def _key_to_f32(key):
    return lax.bitcast_convert_type(jnp.where(key < 0, key ^ 0x7FFFFFFF, key), f32)


def _dsa_kernel(iq_ref, iw_ref, ik_ref, q_ref, k_ref, vt_ref, o_ref, sc_ref, sc16_ref,
                *, tq, n_keep, t_valid):
    i = pl.program_id(1)
    nkb = i + 1
    row = lax.broadcasted_iota(jnp.int32, (tq, tq), 0)
    col = lax.broadcasted_iota(jnp.int32, (tq, tq), 1)
    causal = row <= col
    iw = iw_ref[...]
    kf = float(n_keep)
    int_min = -2 ** 31
    neg_inf_key = int_min + 0x7FFFFF

    def scores(j, masked):
        start = pl.multiple_of(j * tq, tq)
        ikb = ik_ref[pl.ds(start, tq), :]
        sc = jnp.zeros((tq, tq), f32)
        for hh in range(IDX_HEADS):
            s = jnp.dot(ikb, iq_ref[hh], preferred_element_type=f32)
            sc = sc + iw[hh:hh + 1, :] * jnp.maximum(s, 0.0)
        if masked:
            sc = jnp.where(causal, sc, -jnp.inf)
        sc_ref[j] = sc
        bits = lax.bitcast_convert_type(sc, jnp.int32) & -65536
        sc16_ref[j] = lax.bitcast_convert_type(bits, f32).astype(bf16)

    def _scores_body(j, carry):
        scores(j, False)
        return carry

    lax.fori_loop(0, i, _scores_body, 0)
    scores(i, True)

    def count(pred):
        def body(j, acc):
            return acc + jnp.sum(jnp.where(pred(sc_ref[j], j), 1.0, 0.0), axis=0, keepdims=True)
        return lax.fori_loop(0, nkb, body, jnp.zeros((1, tq), f32))

    pack = 16
    one16, zero16 = jnp.ones((), bf16), jnp.zeros((), bf16)

    def count16(c16):
        def body(j, acc):
            ind = jnp.where(sc16_ref[j] >= c16, one16, zero16)
            part = ind[0:pack]
            for t in range(1, tq // pack):
                part = part + ind[t * pack:(t + 1) * pack]
            return acc + jnp.sum(part.astype(f32), axis=0, keepdims=True)
        return lax.fori_loop(0, nkb, body, jnp.zeros((1, tq), f32))

    def accept(cand, cnt, c):
        ok = cnt >= kf
        return jnp.where(ok, cand, c[0]), jnp.where(ok, cnt, c[1])

    def bisect16(it, c):
        cand = c[0] + lax.shift_left(jnp.int32(1), 31 - it)
        c16 = lax.bitcast_convert_type(
            jnp.where(cand < 0, cand ^ 0x7FFFFFFF, cand) & -65536, f32).astype(bf16)
        return accept(cand, count16(c16), c)

    def bisect(it, c):
        cand = c[0] + lax.shift_left(jnp.int32(1), 31 - it)
        cand_f = _key_to_f32(cand)
        return accept(cand, count(lambda blk, j: blk >= cand_f), c)

    c = lax.fori_loop(0, 16, bisect16,
                      (jnp.full((1, tq), int_min, jnp.int32), jnp.zeros((1, tq), f32)))
    ans, n_ge = lax.fori_loop(16, 32, bisect, c)
    thr_key = jnp.maximum(ans, neg_inf_key)
    thr = _key_to_f32(thr_key)

    need = (n_ge > kf) & (ans > neg_inf_key) & (col[0:1, :] + i * tq < t_valid)
    big = float(2 ** 30)
    any_ties = jnp.max(jnp.where(need, 1.0, 0.0)) > 0.0

    def write_bias(select):
        def body(j, carry):
            sc_ref[j] = jnp.where(select(sc_ref[j], j), 0.0, NEG)
            return carry
        lax.fori_loop(0, i, body, 0)
        sc_ref[i] = jnp.where(select(sc_ref[i], i) & causal, 0.0, NEG)

    @pl.when(jnp.logical_not(any_ties))
    def _():
        write_bias(lambda blk, j: blk >= thr)

    @pl.when(any_ties)
    def _():
        budget = kf - count(lambda blk, j: blk > thr)

        def tie_bisect(it, lo):
            cand = lo + lax.shift_left(jnp.int32(1), 14 - it).astype(f32)

            def pred(blk, j):
                keyg = (row + j * tq).astype(f32)
                return jnp.where(blk == thr, keyg, big) < cand
            cnt = count(pred)
            return jnp.where(cnt <= budget - 1.0, cand, lo)

        kstar = lax.fori_loop(0, 15, tie_bisect, jnp.zeros((1, tq), f32))
        kstar = jnp.where(need, kstar, big)
        write_bias(lambda blk, j: (blk > thr)
                   | ((blk == thr) & ((row + j * tq).astype(f32) <= kstar)))

    qs = [q_ref[h] for h in range(HEADS)]

    def scores_att(j):
        start = pl.multiple_of(j * tq, tq)
        b_ = sc_ref[j]
        return tuple(jnp.dot(k_ref[h, pl.ds(start, tq), :], qs[h], preferred_element_type=f32) + b_
                     for h in range(HEADS))

    sm = _flash_blocks(nkb, scores_att, lambda h, j: vt_ref[h, j], None, HEADS, tq, unroll=2)
    for h in range(HEADS):
        l, acc = sm[h]
        o_ref[:, h * HEAD_DIM:(h + 1) * HEAD_DIM] = (acc * (1.0 / l)).T.astype(bf16)


def _dsa(iqt, iwt, ik, qt, k, vt, n_keep, t_valid):
    b, _, nq, _, tq = qt.shape
    assert n_keep <= tq
    tp = nq * tq
    return pl.pallas_call(
        functools.partial(_dsa_kernel, tq=tq, n_keep=n_keep, t_valid=t_valid),
        grid=(b, nq),
        in_specs=[
            pl.BlockSpec((None, None, IDX_HEADS, IDX_DIM, tq), lambda bi, i: (bi, i, 0, 0, 0)),
            pl.BlockSpec((None, None, IDX_HEADS, tq), lambda bi, i: (bi, i, 0, 0)),
            pl.BlockSpec((None, tp, IDX_DIM), lambda bi, i: (bi, 0, 0)),
            pl.BlockSpec((None, HEADS, None, HEAD_DIM, tq), lambda bi, i: (bi, 0, i, 0, 0)),
            pl.BlockSpec((None, HEADS, tp, HEAD_DIM), lambda bi, i: (bi, 0, 0, 0)),
            pl.BlockSpec((None, HEADS, nq, HEAD_DIM, tq), lambda bi, i: (bi, 0, 0, 0, 0)),
        ],
        out_specs=pl.BlockSpec((None, tq, GROUP), lambda bi, i: (bi, i, 0)),
        out_shape=jax.ShapeDtypeStruct((b, tp, GROUP), bf16),
        scratch_shapes=[pltpu.VMEM((nq, tq, tq), f32), pltpu.VMEM((nq, tq, tq), bf16)],
        compiler_params=_cparams(("parallel", "arbitrary")),
        name="dsa",
    )(iqt, iwt, ik, qt, k, vt)


def _rope_tables(tp):
    pos = jnp.arange(tp, dtype=f32)

    def cs(d):
        inv = ROPE_THETA ** (-jnp.arange(0, d, 2, dtype=f32) / d)
        ang = pos[:, None] * inv[None, :]
        return jnp.cos(ang), jnp.sin(ang)

    c, s = cs(64)
    z = jnp.zeros_like(s)
    tabs64 = (jnp.tile(jnp.concatenate([c, c], axis=1), (1, 2)),
              jnp.tile(jnp.concatenate([-s, z], axis=1), (1, 2)),
              jnp.tile(jnp.concatenate([z, s], axis=1), (1, 2)))
    c, s = cs(128)
    tabs128 = (jnp.concatenate([c, c], axis=1), jnp.concatenate([-s, s], axis=1))
    return tabs64, tabs128


def _permute_w_in(w):
    z = lambda n: jnp.zeros(w.shape[:-1] + (n,), w.dtype)
    o_beta = 7 * GROUP
    o_c = o_beta + 2 * HEADS
    o_ik = o_c + 8 * GROUP
    o_iw = o_ik + IDX_DIM
    end = o_iw + IDX_HEADS
    return jnp.concatenate([
        w[..., :o_beta], w[..., o_c:o_ik], w[..., o_ik:o_iw], z(LANES - IDX_DIM),
        w[..., o_iw:end], w[..., o_beta:o_c], z(NP - COL_SM - IDX_HEADS - 2 * HEADS)], axis=-1)


def _lane_row(v, offset):
    return jnp.zeros((1, LANES), f32).at[0, offset:offset + v.shape[0]].set(v.astype(f32))


def kernel(x, meta_tokens, ffn1_norm, ffn1_w_in, ffn1_w_out, mix_norm, w_in, w_out, diff_q_norm, diff_k_norm, diff_lambda_q1, diff_lambda_k1, diff_lambda_q2, diff_lambda_k2, diff_out_norm, gdn_conv_w, gdn_a_log, gdn_dt_bias, gdn_out_norm, ret_out_norm, dsa_q_norm, dsa_k_norm, ffn2_norm, ffn2_w_in, ffn2_w_out):
    b, seq, d = x.shape
    depth = w_in.shape[0]
    n_keep = min(TOPK_MAX, seq // 4)
    t = seq + N_META
    tp = -(-t // ATT_TILE) * ATT_TILE
    rows = b * tp

    h = jnp.concatenate([jnp.broadcast_to(meta_tokens.astype(x.dtype)[None], (b, N_META, d)), x,
                         jnp.zeros((b, tp - t, d), x.dtype)], axis=1).reshape(rows, d)
    tabs64, tabs128 = _rope_tables(tp)
    ret_consts = _retention_consts(ATT_TILE)
    row = lambda v: v.astype(f32)[None, :]
    twice = lambda v: jnp.concatenate([v, v]).astype(f32)[None, :]

    gains = lambda g: g.astype(f32)[:, None, :]
    ffn1 = (gains(ffn1_norm), ffn1_w_in.astype(bf16), ffn1_w_out.astype(bf16))
    ffn2 = (gains(ffn2_norm), ffn2_w_in.astype(bf16), ffn2_w_out.astype(bf16))
    w_in_p = _permute_w_in(w_in).astype(bf16)
    w_out_b = w_out.astype(bf16)
    mix_g = gains(mix_norm)

    for l in range(depth):
        lambda_init = 0.8 - 0.6 * math.exp(-0.3 * l)
        h = _ffn(h, *ffn1, l)
        proj = _proj(h, mix_g, w_in_p, l).reshape(b, tp, NP)

        qz, ka, va = _prep_a(proj, tabs64, twice(diff_q_norm[l]), twice(diff_k_norm[l]))
        y_a = _attn_a(qz, ka, va, row(diff_lambda_q1[l]), row(diff_lambda_k1[l]),
                      row(diff_lambda_q2[l]), row(diff_lambda_k2[l]), row(diff_out_norm[l]),
                      lambda_init)
        y_b = _gdn(proj, gdn_conv_w[l].astype(f32), _lane_row(gdn_a_log[l], SM_ALPHA),
                   _lane_row(gdn_dt_bias[l], SM_ALPHA), row(gdn_out_norm[l]))
        y_c = _retention(proj, tabs64, ret_consts, row(ret_out_norm[l]))
        qd, kd, vd, iq, ik, iw = _prep_d(proj, tabs64, tabs128, row(dsa_q_norm[l]),
                                         row(dsa_k_norm[l]))
        y_d = _dsa(iq, iw, ik, qd, kd, vd, n_keep, t)

        ys = [y.reshape(rows, GROUP) for y in (y_a, y_b, y_c, y_d)]
        h = _outproj(h, ys, w_out_b, l)
        h = _ffn(h, *ffn2, l)
    return h.reshape(b, tp, d)[:, N_META:t]
```
